```python
import math
import jax, jax.numpy as jnp
from jax import lax
import numpy as np

D_MODEL = 1024
BATCH = 2
SEQ = 8192
DEPTH = 2

CHUNK = 64
Q_BLOCK = 128
N_MIXERS = 2
FOX_HEADS = 16
FOX_HEAD_DIM = D_MODEL // FOX_HEADS
HGRN_EXPAND = 128
HGRN_HEADS = D_MODEL // HGRN_EXPAND
HGRN_HEAD_DIM = HGRN_EXPAND
D_FF = -(-8 * D_MODEL // (3 * 256)) * 256
N_FOX = (DEPTH + 1) // 2
N_HGRN = DEPTH // 2
EPS = 1e-6

kernel_name = 'hybrid_fox_hgrn2_trunk'


def rmsnorm(x, g):
    x32 = x.astype(jnp.float32)
    y = x32 * lax.rsqrt(jnp.mean(x32 * x32, axis=-1, keepdims=True) + EPS)
    return (y * g.astype(jnp.float32)).astype(x.dtype)


def fox_attention(h, w_in, b_f, w_out):
    bsz, seq, _ = h.shape
    proj = jnp.einsum('bsd,de->bse', h, w_in)
    q, k, v, f_logit = jnp.split(proj, [D_MODEL, 2 * D_MODEL, 3 * D_MODEL], axis=-1)

    def heads(t):
        return t.reshape(bsz, seq, FOX_HEADS, FOX_HEAD_DIM).transpose(0, 2, 1, 3)

    q = heads(q) * (FOX_HEAD_DIM ** -0.5)
    k = heads(k)
    v = heads(v)
    log_f = jax.nn.log_sigmoid(f_logit.astype(jnp.float32) + b_f.astype(jnp.float32))
    c = jnp.cumsum(log_f, axis=1).transpose(0, 2, 1)
    key_pos = jnp.arange(seq)

    def q_block(blk):
        start = blk * Q_BLOCK
        qb = lax.dynamic_slice_in_dim(q, start, Q_BLOCK, axis=2)
        cb = lax.dynamic_slice_in_dim(c, start, Q_BLOCK, axis=2)
        q_pos = start + jnp.arange(Q_BLOCK)
        logits = jnp.einsum('bhqd,bhkd->bhqk', qb, k).astype(jnp.float32)
        logits = logits + cb[:, :, :, None] - c[:, :, None, :]
        logits = jnp.where(q_pos[:, None] >= key_pos[None, :], logits, -jnp.inf)
        p = jax.nn.softmax(logits, axis=-1).astype(v.dtype)
        return jnp.einsum('bhqk,bhkd->bhqd', p, v)

    out = lax.map(q_block, jnp.arange(seq // Q_BLOCK))
    out = out.transpose(1, 0, 3, 2, 4).reshape(bsz, seq, D_MODEL)
    return jnp.einsum('bsd,de->bse', out, w_out)


def hgrn2_recurrence(h, w_in, lb, g_norm_w, w_out):
    bsz, seq, _ = h.shape
    n_chunks = seq // CHUNK
    proj = jnp.einsum('bsd,de->bse', h, w_in)
    q, f_logit, i_val, g = jnp.split(proj, [D_MODEL, 2 * D_MODEL, 3 * D_MODEL], axis=-1)
    f = lb + (1.0 - lb) * jax.nn.sigmoid(f_logit.astype(jnp.float32))
    log_f = jnp.log(f)
    k = 1.0 - f

    def to_chunks(t):
        t = t.astype(jnp.float32).reshape(bsz, n_chunks, CHUNK, HGRN_HEADS, HGRN_HEAD_DIM)
        return t.transpose(1, 0, 3, 2, 4)

    qc = to_chunks(q)
    kc = to_chunks(k)
    vc = to_chunks(i_val)
    bc = jnp.cumsum(to_chunks(log_f), axis=3)
    causal = jnp.arange(CHUNK)[:, None] >= jnp.arange(CHUNK)[None, :]

    def step(state, xs):
        q_t, k_t, v_t, b_t = xs
        o_inter = jnp.einsum('bhtk,bhkv->bhtv', q_t * jnp.exp(b_t), state)
        diff = b_t[:, :, :, None, :] - b_t[:, :, None, :, :]
        decay = jnp.exp(jnp.where(causal[:, :, None], diff, -jnp.inf))
        scores = jnp.einsum('bhtk,bhsk,bhtsk->bhts', q_t, k_t, decay)
        o = o_inter + jnp.einsum('bhts,bhsv->bhtv', scores, v_t)
        b_last = b_t[:, :, -1, :]
        k_dec = k_t * jnp.exp(b_last[:, :, None, :] - b_t)
        new_state = jnp.exp(b_last)[..., None] * state + jnp.einsum('bhsk,bhsv->bhkv', k_dec, v_t)
        return new_state, o

    state0 = jnp.zeros((bsz, HGRN_HEADS, HGRN_HEAD_DIM, HGRN_HEAD_DIM), jnp.float32)
    _, o = lax.scan(step, state0, (qc, kc, vc, bc))
    o = o.transpose(1, 0, 3, 2, 4).reshape(bsz, seq, HGRN_HEADS, HGRN_HEAD_DIM)
    g = g.astype(jnp.float32).reshape(bsz, seq, HGRN_HEADS, HGRN_HEAD_DIM)
    o = rmsnorm(o, g_norm_w) * jax.nn.silu(g)
    return jnp.einsum('bsd,de->bse', o.reshape(bsz, seq, D_MODEL).astype(h.dtype), w_out)


def swiglu_ffn(h, w_in, w_out):
    gate, up = jnp.split(jnp.einsum('bsd,df->bsf', h, w_in), 2, axis=-1)
    return jnp.einsum('bsf,fd->bsd', jax.nn.silu(gate) * up, w_out)


def setup_inputs(seed: int = 0) -> dict:
    key = jax.random.key(seed)
    ks = jax.random.split(key, 11)
    f32 = jnp.float32

    def normal(k, shape, fan_in):
        return jax.random.normal(k, shape, f32) * (fan_in ** -0.5)

    x = jax.random.normal(ks[0], (BATCH, SEQ, D_MODEL), f32)
    fox_w_in = normal(ks[1], (N_FOX, D_MODEL, 3 * D_MODEL + FOX_HEADS), D_MODEL)
    fox_b_f = jax.random.uniform(ks[2], (N_FOX, FOX_HEADS), f32, 1.0, 4.0)
    fox_w_out = normal(ks[3], (N_FOX, D_MODEL, D_MODEL), D_MODEL)
    hgrn_w_in = normal(ks[4], (N_HGRN, D_MODEL, 4 * D_MODEL), D_MODEL)
    hgrn_lb_table = 0.5 * jax.random.normal(ks[5], (DEPTH + 1, D_MODEL), f32)
    hgrn_gnorm = 1.0 + 0.02 * jax.random.normal(ks[6], (N_HGRN, HGRN_HEAD_DIM), f32)
    hgrn_w_out = normal(ks[7], (N_HGRN, D_MODEL, D_MODEL), D_MODEL)
    ffn_w_in = normal(ks[8], (DEPTH, D_MODEL, 2 * D_FF), D_MODEL)
    ffn_w_out = normal(ks[9], (DEPTH, D_FF, D_MODEL), D_FF)
    norm_gains = 1.0 + 0.02 * jax.random.normal(ks[10], (DEPTH, 4, D_MODEL), f32)
    return {'x': x, 'fox_w_in': fox_w_in, 'fox_b_f': fox_b_f, 'fox_w_out': fox_w_out,
            'hgrn_w_in': hgrn_w_in, 'hgrn_lb_table': hgrn_lb_table, 'hgrn_gnorm': hgrn_gnorm,
            'hgrn_w_out': hgrn_w_out, 'ffn_w_in': ffn_w_in, 'ffn_w_out': ffn_w_out,
            'norm_gains': norm_gains}


def reference(x, fox_w_in, fox_b_f, fox_w_out, hgrn_w_in, hgrn_lb_table, hgrn_gnorm,
              hgrn_w_out, ffn_w_in, ffn_w_out, norm_gains):
    lower_bounds = jnp.cumsum(jax.nn.softmax(hgrn_lb_table.astype(jnp.float32), axis=0), axis=0)
    h = x
    for layer in range(DEPTH):
        j = layer // N_MIXERS
        hn = rmsnorm(h, norm_gains[layer, 0])
        if layer % N_MIXERS == 0:
            mixed = fox_attention(hn, fox_w_in[j], fox_b_f[j], fox_w_out[j])
        else:
            mixed = hgrn2_recurrence(hn, hgrn_w_in[j], lower_bounds[layer], hgrn_gnorm[j],
                                     hgrn_w_out[j])
        h = h + rmsnorm(mixed, norm_gains[layer, 1])
        hn = rmsnorm(h, norm_gains[layer, 2])
        h = h + rmsnorm(swiglu_ffn(hn, ffn_w_in[layer], ffn_w_out[layer]), norm_gains[layer, 3])
    return h
```

```python
import functools

import numpy as np
import jax
import jax.numpy as jnp
from jax import lax
from jax.experimental import pallas as pl
from jax.experimental.pallas import tpu as pltpu

F32 = jnp.float32
BF16 = jnp.bfloat16

EPS = 1e-6
LANES = 128
FOX_HEADS = 16
FOX_HEAD_DIM = 64
HGRN_HEADS = 8
HGRN_HEAD_DIM = 128
HGRN_CHUNK = 64
HGRN_SUB = 16
AUG = 8

ROW_TILE = 512
ATTN_BLOCK = 512
HGRN_BLOCK = 512
FFN_COLS = 256
VMEM_LIMIT = 56 * 1024 * 1024


def _dot(a, b):
    return jnp.dot(a, b, preferred_element_type=F32)


def _dot_nt(a, b):
    return lax.dot_general(a, b, (((1,), (1,)), ((), ())), preferred_element_type=F32)


def _dot_tn(a, b):
    return lax.dot_general(a, b, (((0,), (0,)), ((), ())), preferred_element_type=F32)


def _split3(x):
    hi = x.astype(BF16)
    r = x - hi.astype(F32)
    mid = r.astype(BF16)
    lo = (r - mid.astype(F32)).astype(BF16)
    return hi, mid, lo


def _rms(x, g):
    ms = jnp.mean(x * x, axis=-1, keepdims=True)
    return x * lax.rsqrt(ms + EPS) * g


def _fox_in_kernel(x_ref, g_ref, w_ref, wfh_ref, wfl_ref, bf_ref, tri_ref, psel_ref, pone_ref,
                   qkv_ref, aug_ref, hn_s, carry_s, *, tiles_per_seq, q_scale):
    i = pl.program_id(0)
    j = pl.program_id(1)

    @pl.when(j == 0)
    def _():
        hn = _rms(x_ref[...], g_ref[...])
        hn_hi = hn.astype(BF16)
        hn_lo = (hn - hn_hi.astype(F32)).astype(BF16)
        hn_s[...] = hn_hi
        wfh = wfh_ref[...]
        z = _dot(hn_hi, wfh) + _dot(hn_lo, wfh) + _dot(hn_hi, wfl_ref[...]) + bf_ref[...]
        logf = jnp.minimum(z, 0.0) - jnp.log1p(jnp.exp(-jnp.abs(z)))

        @pl.when(i % tiles_per_seq == 0)
        def _():
            carry_s[...] = jnp.zeros_like(carry_s)

        tri = tri_ref[...]
        l_hi, l_mid, l_lo = _split3(logf)
        c = _dot(tri, l_hi) + _dot(tri, l_mid) + _dot(tri, l_lo) + carry_s[...]
        carry_s[...] = c[c.shape[0] - 1:, :]
        c_hi, c_mid, c_lo = _split3(c)
        ccat = jnp.concatenate([c_hi, c_mid, c_lo], axis=1)
        aug_ref[...] = (_dot(ccat, psel_ref[...]) + pone_ref[...]).astype(aug_ref.dtype)

    acc = _dot(hn_s[...], w_ref[...])
    scale = jnp.where(j == 0, q_scale, 1.0).astype(F32)
    qkv_ref[...] = (acc * scale).astype(qkv_ref.dtype)


def _aug_tables():
    sel = np.zeros((3 * LANES, 2 * LANES), np.float32)
    one = np.zeros((1, 2 * LANES), np.float32)
    for h in range(FOX_HEADS):
        for part in range(3):
            sel[part * LANES + h, AUG * h + part] = 1.0
            sel[part * LANES + h, LANES + AUG * h + 3 + part] = -1.0
            one[0, AUG * h + 3 + part] = 1.0
            one[0, LANES + AUG * h + part] = 1.0
    return jnp.asarray(sel, BF16), jnp.asarray(one, F32)


def _fox_in_proj(h, gain, w_qkv, wf_hi, wf_lo, b_f, seq):
    m, d = h.shape
    tm = ROW_TILE
    n_col = w_qkv.shape[1] // d
    tri = jnp.asarray(np.tril(np.ones((tm, tm), np.float32)), BF16)
    psel, pone = _aug_tables()
    const = lambda i, j: (0, 0)
    return pl.pallas_call(
        functools.partial(_fox_in_kernel, tiles_per_seq=seq // tm, q_scale=FOX_HEAD_DIM ** -0.5),
        grid=(m // tm, n_col),
        in_specs=[
            pl.BlockSpec((tm, d), lambda i, j: (i, 0)),
            pl.BlockSpec((1, d), const),
            pl.BlockSpec((d, d), lambda i, j: (0, j)),
            pl.BlockSpec((d, LANES), const),
            pl.BlockSpec((d, LANES), const),
            pl.BlockSpec((1, LANES), const),
            pl.BlockSpec((tm, tm), const),
            pl.BlockSpec((3 * LANES, 2 * LANES), const),
            pl.BlockSpec((1, 2 * LANES), const),
        ],
        out_specs=[
            pl.BlockSpec((tm, d), lambda i, j: (i, j)),
            pl.BlockSpec((tm, 2 * LANES), lambda i, j: (i, 0)),
        ],
        out_shape=[
            jax.ShapeDtypeStruct((m, n_col * d), BF16),
            jax.ShapeDtypeStruct((m, 2 * LANES), BF16),
        ],
        scratch_shapes=[pltpu.VMEM((tm, d), BF16), pltpu.VMEM((1, LANES), F32)],
        compiler_params=pltpu.CompilerParams(
            dimension_semantics=("arbitrary", "arbitrary"), vmem_limit_bytes=VMEM_LIMIT),
        name="fox_in_proj",
    )(h, gain, w_qkv, wf_hi, wf_lo, b_f, tri, psel, pone)


def _fox_attn_kernel(q_ref, augq_ref, k_ref, augk_ref, v_ref, o_ref, *, blk):
    hp = pl.program_id(1)
    qi = pl.program_id(2)
    lane = lax.broadcasted_iota(jnp.int32, (blk, LANES), 1)
    q2 = q_ref[0]
    aq = augq_ref[0]
    zero = jnp.zeros_like(q2)
    qcat = []
    for half in range(2):
        qm = jnp.where(lane // FOX_HEAD_DIM == half, q2, zero)
        am = jnp.where(lane // AUG == 2 * hp + half, aq, zero)
        qcat.append(jnp.concatenate([qm, am], axis=1))

    def step(kb, carry, masked):
        k0 = pl.multiple_of(kb * blk, blk)
        kcat = jnp.concatenate([k_ref[0, pl.ds(k0, blk), :], augk_ref[0, pl.ds(k0, blk), :]], axis=1)
        vv = v_ref[0, pl.ds(k0, blk), :]
        new = []
        for half in range(2):
            m_prev, l_prev, acc_prev = carry[half]
            s = _dot_nt(qcat[half], kcat)
            if masked:
                row = lax.broadcasted_iota(jnp.int32, s.shape, 0)
                col = lax.broadcasted_iota(jnp.int32, s.shape, 1)
                s = jnp.where(row >= col, s, -jnp.inf)
            m_new = jnp.maximum(m_prev, jnp.max(s, axis=1, keepdims=True))
            alpha = jnp.exp(m_prev - m_new)
            p = jnp.exp(s - m_new)
            l_new = alpha * l_prev + jnp.sum(p, axis=1, keepdims=True)
            acc_new = alpha * acc_prev + _dot(p.astype(BF16), vv)
            new.append((m_new, l_new, acc_new))
        return tuple(new)

    init = tuple((jnp.full((blk, 1), -jnp.inf, F32), jnp.zeros((blk, 1), F32),
                  jnp.zeros((blk, LANES), F32)) for _ in range(2))
    carry = lax.fori_loop(0, qi, lambda kb, c: step(kb, c, False), init)
    carry = step(qi, carry, True)
    out = [acc / l for (_, l, acc) in carry]
    o_ref[0] = jnp.where(lane // FOX_HEAD_DIM == 0, out[0], out[1]).astype(o_ref.dtype)


def _fox_attention(qkv, aug, bsz, seq, d):
    blk = ATTN_BLOCK
    n_pair = d // LANES
    qkv3 = qkv.reshape(bsz, seq, 3 * d)
    aug3 = aug.reshape(bsz, seq, 2 * LANES)
    return pl.pallas_call(
        functools.partial(_fox_attn_kernel, blk=blk),
        grid=(bsz, n_pair, seq // blk),
        in_specs=[
            pl.BlockSpec((1, blk, LANES), lambda b, hp, qi: (b, qi, hp)),
            pl.BlockSpec((1, blk, LANES), lambda b, hp, qi: (b, qi, 0)),
            pl.BlockSpec((1, seq, LANES), lambda b, hp, qi: (b, 0, n_pair + hp)),
            pl.BlockSpec((1, seq, LANES), lambda b, hp, qi: (b, 0, 1)),
            pl.BlockSpec((1, seq, LANES), lambda b, hp, qi: (b, 0, 2 * n_pair + hp)),
        ],
        out_specs=pl.BlockSpec((1, blk, LANES), lambda b, hp, qi: (b, qi, hp)),
        out_shape=jax.ShapeDtypeStruct((bsz, seq, d), BF16),
        compiler_params=pltpu.CompilerParams(
            dimension_semantics=("arbitrary", "arbitrary", "arbitrary"),
            vmem_limit_bytes=VMEM_LIMIT),
        name="fox_attention",
    )(qkv3, aug3, qkv3, aug3, qkv3).reshape(bsz * seq, d)


def _out_proj_kernel(a_ref, w_ref, h_ref, g_post_ref, g_pre_ref, h_out_ref, hn_out_ref):
    y = _dot(a_ref[...], w_ref[...])
    h1 = h_ref[...] + _rms(y, g_post_ref[...])
    h_out_ref[...] = h1
    hn_out_ref[...] = _rms(h1, g_pre_ref[...]).astype(hn_out_ref.dtype)


def _out_proj(a, w, h, g_post, g_pre):
    m, d = h.shape
    tm = ROW_TILE
    const = lambda i: (0, 0)
    row = lambda i: (i, 0)
    return pl.pallas_call(
        _out_proj_kernel,
        grid=(m // tm,),
        in_specs=[
            pl.BlockSpec((tm, d), row),
            pl.BlockSpec((d, d), const),
            pl.BlockSpec((tm, d), row),
            pl.BlockSpec((1, d), const),
            pl.BlockSpec((1, d), const),
        ],
        out_specs=[pl.BlockSpec((tm, d), row), pl.BlockSpec((tm, d), row)],
        out_shape=[jax.ShapeDtypeStruct((m, d), F32), jax.ShapeDtypeStruct((m, d), BF16)],
        compiler_params=pltpu.CompilerParams(
            dimension_semantics=("arbitrary",), vmem_limit_bytes=VMEM_LIMIT),
        name="out_proj",
    )(a, w, h, g_post, g_pre)


def _ffn_kernel(x_ref, wg_ref, wu_ref, wo_ref, h_ref, g_ref, out_ref, *, d_ff):
    x = x_ref[...]
    acc = None
    for c0 in range(0, d_ff, FFN_COLS):
        gate = _dot(x, wg_ref[:, c0:c0 + FFN_COLS])
        up = _dot(x, wu_ref[:, c0:c0 + FFN_COLS])
        act = (gate * jax.nn.sigmoid(gate) * up).astype(BF16)
        part = _dot(act, wo_ref[c0:c0 + FFN_COLS, :])
        acc = part if acc is None else acc + part
    out_ref[...] = h_ref[...] + _rms(acc, g_ref[...])


def _ffn(hn, w_in, w_out, h, gain):
    m, d = h.shape
    d_ff = w_out.shape[0]
    tm = ROW_TILE
    row = lambda i: (i, 0)
    resident = pl.Buffered(1)
    return pl.pallas_call(
        functools.partial(_ffn_kernel, d_ff=d_ff),
        grid=(m // tm,),
        in_specs=[
            pl.BlockSpec((tm, d), row),
            pl.BlockSpec((d, d_ff), lambda i: (0, 0), pipeline_mode=resident),
            pl.BlockSpec((d, d_ff), lambda i: (0, 1), pipeline_mode=resident),
            pl.BlockSpec((d_ff, d), lambda i: (0, 0), pipeline_mode=resident),
            pl.BlockSpec((tm, d), row),
            pl.BlockSpec((1, d), lambda i: (0, 0)),
        ],
        out_specs=pl.BlockSpec((tm, d), row),
        out_shape=jax.ShapeDtypeStruct((m, d), F32),
        compiler_params=pltpu.CompilerParams(
            dimension_semantics=("arbitrary",), vmem_limit_bytes=VMEM_LIMIT),
        name="ffn",
    )(hn, w_in, w_in, w_out, h, gain)


def _hgrn_in_kernel(x_ref, g_ref, w_ref, tab_ref, proj_ref, f_ref, hn_s, *, layer):
    j = pl.program_id(1)

    @pl.when(j == 0)
    def _():
        hn_s[...] = _rms(x_ref[...], g_ref[...]).astype(BF16)

    acc = _dot(hn_s[...], w_ref[...])
    proj_ref[...] = acc.astype(proj_ref.dtype)

    @pl.when(j == 1)
    def _():
        tab = tab_ref[...]
        e = jnp.exp(tab - jnp.max(tab, axis=0, keepdims=True))
        lb = jnp.sum(e[:layer + 1], axis=0, keepdims=True) / jnp.sum(e, axis=0, keepdims=True)
        f_ref[...] = lb + (1.0 - lb) * jax.nn.sigmoid(acc)


def _hgrn_in_proj(h, gain, w, table, layer):
    m, d = h.shape
    tm = ROW_TILE
    n_col = w.shape[1] // d
    const = lambda i, j: (0, 0)
    return pl.pallas_call(
        functools.partial(_hgrn_in_kernel, layer=layer),
        grid=(m // tm, n_col),
        in_specs=[
            pl.BlockSpec((tm, d), lambda i, j: (i, 0)),
            pl.BlockSpec((1, d), const),
            pl.BlockSpec((d, d), lambda i, j: (0, j)),
            pl.BlockSpec(table.shape, const),
        ],
        out_specs=[
            pl.BlockSpec((tm, d), lambda i, j: (i, j)),
            pl.BlockSpec((tm, d), lambda i, j: (i, 0)),
        ],
        out_shape=[
            jax.ShapeDtypeStruct((m, n_col * d), BF16),
            jax.ShapeDtypeStruct((m, d), F32),
        ],
        scratch_shapes=[pltpu.VMEM((tm, d), BF16)],
        compiler_params=pltpu.CompilerParams(
            dimension_semantics=("arbitrary", "arbitrary"), vmem_limit_bytes=VMEM_LIMIT),
        name="hgrn_in_proj",
    )(h, gain, w, table)


def _hgrn_kernel(q_ref, f_ref, i_ref, g_ref, gn_ref, tri_ref, ones_ref, o_ref,
                 st_s, b_s, k_s, v_s, q_s, *, blk):
    ch, sub = HGRN_CHUNK, HGRN_SUB
    n_chunk = blk // ch
    n_sub = ch // sub

    @pl.when(pl.program_id(2) == 0)
    def _():
        st_s[...] = jnp.zeros_like(st_s)

    f = f_ref[0]
    logf = jnp.log(f)
    k_s[...] = 1.0 - f
    v_s[...] = i_ref[0].astype(F32)
    q_s[...] = q_ref[0].astype(F32)
    lcat = jnp.concatenate([logf[c * ch:(c + 1) * ch] for c in range(n_chunk)], axis=1)
    l_hi, l_mid, l_lo = _split3(lcat)
    tri = tri_ref[...]
    bcat = _dot(tri, l_hi) + _dot(tri, l_mid) + _dot(tri, l_lo)
    for c in range(n_chunk):
        b_s[c * ch:(c + 1) * ch, :] = bcat[:, c * LANES:(c + 1) * LANES]

    ones = ones_ref[...]
    gn = gn_ref[...]
    sub_row = lax.broadcasted_iota(jnp.int32, (sub, LANES), 0)
    ch_row = lax.broadcasted_iota(jnp.int32, (ch, LANES), 0)

    def chunk(c, carry):
        r0 = pl.multiple_of(c * ch, ch)
        b = b_s[pl.ds(r0, ch), :]
        kk = k_s[pl.ds(r0, ch), :]
        v = v_s[pl.ds(r0, ch), :]
        q = q_s[pl.ds(r0, ch), :]
        st = st_s[...]
        v_bf = v.astype(BF16)
        b_last = b_s[pl.ds(r0 + ch - 1, 1), :]

        o = _dot_nt((q * jnp.exp(b)).astype(BF16), st.astype(BF16))

        k_dec = (kk * jnp.exp(b_last - b)).astype(BF16)
        st_s[...] = jnp.exp(b_last) * st + _dot_tn(v_bf, k_dec)

        a_rows = [jnp.zeros((sub, ch), F32)]
        for i in range(1, n_sub):
            ref = b_s[pl.ds(r0 + sub * i - 1, 1), :]
            qt = q[sub * i:sub * (i + 1)] * jnp.exp(b[sub * i:sub * (i + 1)] - ref)
            kt = jnp.where(ch_row < sub * i, kk * jnp.exp(jnp.minimum(ref - b, 0.0)), 0.0)
            a_rows.append(_dot_nt(qt.astype(BF16), kt.astype(BF16)))
        a_off = jnp.concatenate(a_rows, axis=0)
        o = o + _dot(a_off.astype(BF16), v_bf)

        xs = []
        for i in range(n_sub):
            bi = b[sub * i:sub * (i + 1)]
            qi = q[sub * i:sub * (i + 1)]
            for s in range(sub):
                r = r0 + sub * i + s
                bs = b_s[pl.ds(r, 1), :]
                ks = k_s[pl.ds(r, 1), :]
                e = jnp.exp(jnp.where(sub_row >= s, bi - bs, -jnp.inf))
                xs.append((qi * (ks * e)).astype(BF16))
        y = _dot(jnp.concatenate(xs, axis=0), ones)
        o_diag = []
        for i in range(n_sub):
            acc = jnp.zeros((sub, LANES), F32)
            for s in range(sub):
                n = sub * i + s
                acc = acc + y[n * sub:(n + 1) * sub] * v_s[pl.ds(r0 + n, 1), :]
            o_diag.append(acc)
        o = o + jnp.concatenate(o_diag, axis=0)

        g = g_ref[0, pl.ds(r0, ch), :].astype(F32)
        o_ref[0, pl.ds(r0, ch), :] = (_rms(o, gn) * (g * jax.nn.sigmoid(g))).astype(o_ref.dtype)
        return carry

    lax.fori_loop(0, n_chunk, chunk, 0)


def _hgrn_recurrence(proj, f, gnorm, bsz, seq, d):
    blk = HGRN_BLOCK
    nh = d // HGRN_HEAD_DIM
    proj3 = proj.reshape(bsz, seq, 4 * d)
    f3 = f.reshape(bsz, seq, d)
    tri = jnp.asarray(np.tril(np.ones((HGRN_CHUNK, HGRN_CHUNK), np.float32)), BF16)
    ones = jnp.ones((LANES, LANES), BF16)
    const = lambda b, h, t: (0, 0)
    tok = lambda col: (lambda b, h, t: (b, t, col * nh + h))
    return pl.pallas_call(
        functools.partial(_hgrn_kernel, blk=blk),
        grid=(bsz, nh, seq // blk),
        in_specs=[
            pl.BlockSpec((1, blk, LANES), tok(0)),
            pl.BlockSpec((1, blk, LANES), tok(0)),
            pl.BlockSpec((1, blk, LANES), tok(2)),
            pl.BlockSpec((1, blk, LANES), tok(3)),
            pl.BlockSpec((1, LANES), const),
            pl.BlockSpec((HGRN_CHUNK, HGRN_CHUNK), const),
            pl.BlockSpec((LANES, LANES), const),
        ],
        out_specs=pl.BlockSpec((1, blk, LANES), tok(0)),
        out_shape=jax.ShapeDtypeStruct((bsz, seq, d), BF16),
        scratch_shapes=[pltpu.VMEM((LANES, LANES), F32)] + [pltpu.VMEM((blk, LANES), F32)] * 4,
        compiler_params=pltpu.CompilerParams(
            dimension_semantics=("arbitrary", "arbitrary", "arbitrary"),
            vmem_limit_bytes=VMEM_LIMIT),
        name="hgrn_recurrence",
    )(proj3, f3, proj3, proj3, gnorm, tri, ones).reshape(bsz * seq, d)


def kernel(x, fox_w_in, fox_b_f, fox_w_out, hgrn_w_in, hgrn_lb_table, hgrn_gnorm, hgrn_w_out,
           ffn_w_in, ffn_w_out, norm_gains):
    bsz, seq, d = x.shape
    depth = norm_gains.shape[0]
    assert d == FOX_HEADS * FOX_HEAD_DIM == HGRN_HEADS * HGRN_HEAD_DIM
    assert seq % ROW_TILE == 0 and seq % ATTN_BLOCK == 0 and seq % HGRN_BLOCK == 0
    assert ffn_w_out.shape[1] % FFN_COLS == 0
    gains = norm_gains.astype(F32).reshape(depth, 4, 1, d)
    h = x.reshape(bsz * seq, d).astype(F32)

    for layer in range(depth):
        j = layer // 2
        if layer % 2 == 0:
            w_in = fox_w_in[j]
            w_f = jnp.pad(w_in[:, 3 * d:].astype(F32), ((0, 0), (0, LANES - FOX_HEADS)))
            wf_hi = w_f.astype(BF16)
            wf_lo = (w_f - wf_hi.astype(F32)).astype(BF16)
            b_f = jnp.pad(fox_b_f[j].astype(F32), (0, LANES - FOX_HEADS)).reshape(1, LANES)
            qkv, aug = _fox_in_proj(h, gains[layer, 0], w_in[:, :3 * d].astype(BF16),
                                    wf_hi, wf_lo, b_f, seq)
            mixed = _fox_attention(qkv, aug, bsz, seq, d)
            w_out = fox_w_out[j]
        else:
            proj, f = _hgrn_in_proj(h, gains[layer, 0], hgrn_w_in[j].astype(BF16),
                                    hgrn_lb_table.astype(F32), layer)
            mixed = _hgrn_recurrence(proj, f, hgrn_gnorm[j].astype(F32).reshape(1, -1),
                                     bsz, seq, d)
            w_out = hgrn_w_out[j]
        h, hn = _out_proj(mixed, w_out.astype(BF16), h, gains[layer, 1], gains[layer, 2])
        h = _ffn(hn, ffn_w_in[layer].astype(BF16), ffn_w_out[layer].astype(BF16), h,
                 gains[layer, 3])
    return h.reshape(bsz, seq, d).astype(x.dtype)
```

```python
import functools

import numpy as np
import jax
import jax.numpy as jnp
from jax import lax
from jax.experimental import pallas as pl
from jax.experimental.pallas import tpu as pltpu

F32 = jnp.float32
BF16 = jnp.bfloat16

EPS = 1e-6
LANES = 128
FOX_HEADS = 16
FOX_HEAD_DIM = 64
HGRN_HEADS = 8
HGRN_HEAD_DIM = 128
HGRN_CHUNK = 64
HGRN_SUB = 16
AUG = 8
ONES_ROWS = 16
LOG2E = 1.4426950408889634

ROW_TILE = 512
ATTN_BLOCK = 512
HGRN_BLOCK = 512
FFN_COLS = 256
VMEM_LIMIT = 56 * 1024 * 1024


def _dot(a, b):
    return jnp.dot(a, b, preferred_element_type=F32)


def _dot_nt(a, b):
    return lax.dot_general(a, b, (((1,), (1,)), ((), ())), preferred_element_type=F32)


def _dot_tn(a, b):
    return lax.dot_general(a, b, (((0,), (0,)), ((), ())), preferred_element_type=F32)


def _split3(x):
    hi = x.astype(BF16)
    r = x - hi.astype(F32)
    mid = r.astype(BF16)
    lo = (r - mid.astype(F32)).astype(BF16)
    return hi, mid, lo


def _rms(x, g):
    ms = jnp.mean(x * x, axis=-1, keepdims=True)
    return x * lax.rsqrt(ms + EPS) * g


def _fox_in_kernel(x_ref, g_ref, w_ref, wvt_ref, wfh_ref, wfl_ref, bf_ref, tri_ref, psel_ref,
                   pone_ref, qk_ref, vt_ref, aug_ref, hn_s, carry_s, *, tiles_per_seq, q_scale):
    i = pl.program_id(0)
    j = pl.program_id(1)

    @pl.when(j == 0)
    def _():
        hn = _rms(x_ref[...], g_ref[...])
        hn_hi = hn.astype(BF16)
        hn_lo = (hn - hn_hi.astype(F32)).astype(BF16)
        hn_s[...] = hn_hi
        wfh = wfh_ref[...]
        z = _dot(hn_hi, wfh) + _dot(hn_lo, wfh) + _dot(hn_hi, wfl_ref[...]) + bf_ref[...]
        logf = jnp.minimum(z, 0.0) - jnp.log1p(jnp.exp(-jnp.abs(z)))

        @pl.when(i % tiles_per_seq == 0)
        def _():
            carry_s[...] = jnp.zeros_like(carry_s)

        tri = tri_ref[...]
        l_hi, l_mid, l_lo = _split3(logf)
        c = _dot(tri, l_hi) + _dot(tri, l_mid) + _dot(tri, l_lo) + carry_s[...]
        carry_s[...] = c[c.shape[0] - 1:, :]
        c_hi, c_mid, c_lo = _split3(c * LOG2E)
        ccat = jnp.concatenate([c_hi, c_mid, c_lo], axis=1)
        aug_ref[...] = (_dot(ccat, psel_ref[...]) + pone_ref[...]).astype(aug_ref.dtype)

    @pl.when(j < 2)
    def _():
        acc = _dot(hn_s[...], w_ref[...])
        scale = jnp.where(j == 0, q_scale, 1.0).astype(F32)
        qk_ref[...] = (acc * scale).astype(qk_ref.dtype)

    @pl.when(j == 2)
    def _():
        vt_ref[...] = _dot_nt(wvt_ref[...], hn_s[...]).astype(vt_ref.dtype)


def _aug_tables():
    sel = np.zeros((3 * LANES, 2 * LANES), np.float32)
    one = np.zeros((1, 2 * LANES), np.float32)
    for h in range(FOX_HEADS):
        for part in range(3):
            sel[part * LANES + h, AUG * h + part] = 1.0
            sel[part * LANES + h, LANES + AUG * h + 3 + part] = -1.0
            one[0, AUG * h + 3 + part] = 1.0
            one[0, LANES + AUG * h + part] = 1.0
    return jnp.asarray(sel, BF16), jnp.asarray(one, F32)


def _fox_in_proj(h, gain, w_qk, w_vt, wf_hi, wf_lo, b_f, seq):
    m, d = h.shape
    tm = ROW_TILE
    tri = jnp.asarray(np.tril(np.ones((tm, tm), np.float32)), BF16)
    psel, pone = _aug_tables()
    const = lambda i, j: (0, 0)
    qk_col = lambda i, j: (i, jnp.minimum(j, 1))
    return pl.pallas_call(
        functools.partial(_fox_in_kernel, tiles_per_seq=seq // tm,
                          q_scale=FOX_HEAD_DIM ** -0.5 * LOG2E),
        grid=(m // tm, 3),
        in_specs=[
            pl.BlockSpec((tm, d), lambda i, j: (i, 0)),
            pl.BlockSpec((1, d), const),
            pl.BlockSpec((d, d), lambda i, j: (0, jnp.minimum(j, 1))),
            pl.BlockSpec((d, d), const),
            pl.BlockSpec((d, LANES), const),
            pl.BlockSpec((d, LANES), const),
            pl.BlockSpec((1, LANES), const),
            pl.BlockSpec((tm, tm), const),
            pl.BlockSpec((3 * LANES, 2 * LANES), const),
            pl.BlockSpec((1, 2 * LANES), const),
        ],
        out_specs=[
            pl.BlockSpec((tm, d), qk_col),
            pl.BlockSpec((d, tm), lambda i, j: (0, i)),
            pl.BlockSpec((tm, 2 * LANES), lambda i, j: (i, 0)),
        ],
        out_shape=[
            jax.ShapeDtypeStruct((m, 2 * d), BF16),
            jax.ShapeDtypeStruct((d, m), BF16),
            jax.ShapeDtypeStruct((m, 2 * LANES), BF16),
        ],
        scratch_shapes=[pltpu.VMEM((tm, d), BF16), pltpu.VMEM((1, LANES), F32)],
        compiler_params=pltpu.CompilerParams(
            dimension_semantics=("arbitrary", "arbitrary"), vmem_limit_bytes=VMEM_LIMIT),
        name="fox_in_proj",
    )(h, gain, w_qk, w_vt, wf_hi, wf_lo, b_f, tri, psel, pone)


def _fox_attn_kernel(q_ref, augq_ref, k_ref, augk_ref, vt_ref, o_ref,
                     qcat_s, s0_s, s1_s, mx0_s, mx1_s, m_s, acc_s, *, blk):
    hp = pl.program_id(1)
    g = pl.program_id(2)
    nq = 2 * blk
    hd = FOX_HEAD_DIM
    lane = lax.broadcasted_iota(jnp.int32, (nq, LANES), 1)
    q2 = q_ref[0]
    aq = augq_ref[0]
    zero = jnp.zeros_like(q2)
    for half in range(2):
        qm = jnp.where(lane // hd == half, q2, zero)
        am = jnp.where(lane // AUG == 2 * hp + half, aq, zero)
        qcat_s[half] = jnp.concatenate([qm, am], axis=1)
    m_s[...] = jnp.full(m_s.shape, -jnp.inf, F32)
    acc_s[...] = jnp.zeros(acc_s.shape, F32)
    ones_rows = jnp.ones((ONES_ROWS, blk), BF16)

    def scores(kb, s_ref, mx_ref, col0, ncol):
        k0 = pl.multiple_of(kb * blk, blk)
        kcat = jnp.concatenate([k_ref[0, pl.ds(k0, blk), :], augk_ref[0, pl.ds(k0, blk), :]], axis=1)
        for half in range(2):
            st = _dot_nt(kcat, qcat_s[half, col0:col0 + ncol, :])
            s_ref[half, :, col0:col0 + ncol] = st
            if mx_ref is not None:
                mx_ref[half] = jnp.max(st, axis=0, keepdims=True)

    def update(kb, s_ref, mx_ref, col0, ncol):
        k0 = pl.multiple_of(kb * blk, blk)
        for half in range(2):
            st = s_ref[half, :, col0:col0 + ncol]
            if mx_ref is None:
                krow = lax.broadcasted_iota(jnp.int32, st.shape, 0)
                qcol = lax.broadcasted_iota(jnp.int32, st.shape, 1)
                st = jnp.where(krow <= qcol, st, -jnp.inf)
                bmax = jnp.max(st, axis=0, keepdims=True)
            else:
                bmax = mx_ref[half]
            m_prev = m_s[half, :, col0:col0 + ncol]
            m_new = jnp.maximum(m_prev, bmax)
            alpha = jnp.exp2(m_prev - m_new)
            pt = jnp.exp2(st - m_new).astype(BF16)
            vaug = jnp.concatenate([vt_ref[hd * half:hd * (half + 1), pl.ds(k0, blk)], ones_rows],
                                   axis=0)
            acc_s[half, :, col0:col0 + ncol] = (alpha * acc_s[half, :, col0:col0 + ncol]
                                                + _dot(vaug, pt))
            m_s[half, :, col0:col0 + ncol] = m_new

    scores(0, s0_s, mx0_s, 0, nq)

    def pair(t, carry):
        scores(2 * t + 1, s1_s, mx1_s, 0, nq)
        update(2 * t, s0_s, mx0_s, 0, nq)
        scores(2 * t + 2, s0_s, mx0_s, 0, nq)
        update(2 * t + 1, s1_s, mx1_s, 0, nq)
        return carry

    lax.fori_loop(0, g, pair, 0)
    scores(2 * g + 1, s1_s, None, blk, blk)
    update(2 * g, s0_s, None, 0, nq)
    update(2 * g + 1, s1_s, None, blk, blk)

    outs = []
    for half in range(2):
        acc = acc_s[half]
        outs.append(acc[:hd] / acc[hd:hd + 1])
    o_ref[0] = jnp.concatenate(outs, axis=0).T.astype(o_ref.dtype)


def _fox_attention(qk, vt, aug, bsz, seq, d):
    blk = ATTN_BLOCK
    nq = 2 * blk
    n_pair = d // LANES
    qk3 = qk.reshape(bsz, seq, 2 * d)
    aug3 = aug.reshape(bsz, seq, 2 * LANES)
    return pl.pallas_call(
        functools.partial(_fox_attn_kernel, blk=blk),
        grid=(bsz, n_pair, seq // nq),
        in_specs=[
            pl.BlockSpec((1, nq, LANES), lambda b, hp, g: (b, g, hp)),
            pl.BlockSpec((1, nq, LANES), lambda b, hp, g: (b, g, 0)),
            pl.BlockSpec((1, seq, LANES), lambda b, hp, g: (b, 0, n_pair + hp)),
            pl.BlockSpec((1, seq, LANES), lambda b, hp, g: (b, 0, 1)),
            pl.BlockSpec((LANES, seq), lambda b, hp, g: (hp, b)),
        ],
        out_specs=pl.BlockSpec((1, nq, LANES), lambda b, hp, g: (b, g, hp)),
        out_shape=jax.ShapeDtypeStruct((bsz, seq, d), BF16),
        scratch_shapes=[
            pltpu.VMEM((2, nq, 2 * LANES), BF16),
            pltpu.VMEM((2, blk, nq), F32),
            pltpu.VMEM((2, blk, nq), F32),
            pltpu.VMEM((2, 1, nq), F32),
            pltpu.VMEM((2, 1, nq), F32),
            pltpu.VMEM((2, 1, nq), F32),
            pltpu.VMEM((2, FOX_HEAD_DIM + ONES_ROWS, nq), F32),
        ],
        compiler_params=pltpu.CompilerParams(
            dimension_semantics=("arbitrary", "arbitrary", "arbitrary"),
            vmem_limit_bytes=VMEM_LIMIT),
        name="fox_attention",
    )(qk3, aug3, qk3, aug3, vt).reshape(bsz * seq, d)


def _out_proj_kernel(a_ref, w_ref, h_ref, g_post_ref, g_pre_ref, h_out_ref, hn_out_ref):
    y = _dot(a_ref[...], w_ref[...])
    h1 = h_ref[...] + _rms(y, g_post_ref[...])
    h_out_ref[...] = h1
    hn_out_ref[...] = _rms(h1, g_pre_ref[...]).astype(hn_out_ref.dtype)


def _out_proj(a, w, h, g_post, g_pre):
    m, d = h.shape
    tm = ROW_TILE
    const = lambda i: (0, 0)
    row = lambda i: (i, 0)
    return pl.pallas_call(
        _out_proj_kernel,
        grid=(m // tm,),
        in_specs=[
            pl.BlockSpec((tm, d), row),
            pl.BlockSpec((d, d), const),
            pl.BlockSpec((tm, d), row),
            pl.BlockSpec((1, d), const),
            pl.BlockSpec((1, d), const),
        ],
        out_specs=[pl.BlockSpec((tm, d), row), pl.BlockSpec((tm, d), row)],
        out_shape=[jax.ShapeDtypeStruct((m, d), F32), jax.ShapeDtypeStruct((m, d), BF16)],
        compiler_params=pltpu.CompilerParams(
            dimension_semantics=("arbitrary",), vmem_limit_bytes=VMEM_LIMIT),
        name="out_proj",
    )(a, w, h, g_post, g_pre)


def _ffn_kernel(x_ref, wg_ref, wu_ref, wo_ref, h_ref, g_ref, out_ref, *, d_ff):
    x = x_ref[...]
    acc = None
    for c0 in range(0, d_ff, FFN_COLS):
        gate = _dot(x, wg_ref[:, c0:c0 + FFN_COLS])
        up = _dot(x, wu_ref[:, c0:c0 + FFN_COLS])
        act = (gate * jax.nn.sigmoid(gate) * up).astype(BF16)
        part = _dot(act, wo_ref[c0:c0 + FFN_COLS, :])
        acc = part if acc is None else acc + part
    out_ref[...] = h_ref[...] + _rms(acc, g_ref[...])


def _ffn(hn, w_in, w_out, h, gain):
    m, d = h.shape
    d_ff = w_out.shape[0]
    tm = ROW_TILE
    row = lambda i: (i, 0)
    resident = pl.Buffered(1)
    return pl.pallas_call(
        functools.partial(_ffn_kernel, d_ff=d_ff),
        grid=(m // tm,),
        in_specs=[
            pl.BlockSpec((tm, d), row),
            pl.BlockSpec((d, d_ff), lambda i: (0, 0), pipeline_mode=resident),
            pl.BlockSpec((d, d_ff), lambda i: (0, 1), pipeline_mode=resident),
            pl.BlockSpec((d_ff, d), lambda i: (0, 0), pipeline_mode=resident),
            pl.BlockSpec((tm, d), row),
            pl.BlockSpec((1, d), lambda i: (0, 0)),
        ],
        out_specs=pl.BlockSpec((tm, d), row),
        out_shape=jax.ShapeDtypeStruct((m, d), F32),
        compiler_params=pltpu.CompilerParams(
            dimension_semantics=("arbitrary",), vmem_limit_bytes=VMEM_LIMIT),
        name="ffn",
    )(hn, w_in, w_in, w_out, h, gain)


def _hgrn_in_kernel(x_ref, g_ref, w_ref, tab_ref, proj_ref, f_ref, hn_s, *, layer):
    j = pl.program_id(1)

    @pl.when(j == 0)
    def _():
        hn_s[...] = _rms(x_ref[...], g_ref[...]).astype(BF16)

    acc = _dot(hn_s[...], w_ref[...])
    proj_ref[...] = acc.astype(proj_ref.dtype)

    @pl.when(j == 1)
    def _():
        tab = tab_ref[...]
        e = jnp.exp(tab - jnp.max(tab, axis=0, keepdims=True))
        lb = jnp.sum(e[:layer + 1], axis=0, keepdims=True) / jnp.sum(e, axis=0, keepdims=True)
        f_ref[...] = lb + (1.0 - lb) * jax.nn.sigmoid(acc)


def _hgrn_in_proj(h, gain, w, table, layer):
    m, d = h.shape
    tm = ROW_TILE
    n_col = w.shape[1] // d
    const = lambda i, j: (0, 0)
    return pl.pallas_call(
        functools.partial(_hgrn_in_kernel, layer=layer),
        grid=(m // tm, n_col),
        in_specs=[
            pl.BlockSpec((tm, d), lambda i, j: (i, 0)),
            pl.BlockSpec((1, d), const),
            pl.BlockSpec((d, d), lambda i, j: (0, j)),
            pl.BlockSpec(table.shape, const),
        ],
        out_specs=[
            pl.BlockSpec((tm, d), lambda i, j: (i, j)),
            pl.BlockSpec((tm, d), lambda i, j: (i, 0)),
        ],
        out_shape=[
            jax.ShapeDtypeStruct((m, n_col * d), BF16),
            jax.ShapeDtypeStruct((m, d), F32),
        ],
        scratch_shapes=[pltpu.VMEM((tm, d), BF16)],
        compiler_params=pltpu.CompilerParams(
            dimension_semantics=("arbitrary", "arbitrary"), vmem_limit_bytes=VMEM_LIMIT),
        name="hgrn_in_proj",
    )(h, gain, w, table)


def _hgrn_kernel(q_ref, f_ref, i_ref, g_ref, gn_ref, tri_ref, ones_ref, o_ref,
                 st_s, b_s, k_s, v_s, q_s, *, blk):
    ch, sub = HGRN_CHUNK, HGRN_SUB
    n_chunk = blk // ch
    n_sub = ch // sub

    @pl.when(pl.program_id(2) == 0)
    def _():
        st_s[...] = jnp.zeros_like(st_s)

    f = f_ref[0]
    logf = jnp.log(f)
    k_s[...] = 1.0 - f
    v_s[...] = i_ref[0].astype(F32)
    q_s[...] = q_ref[0].astype(F32)
    lcat = jnp.concatenate([logf[c * ch:(c + 1) * ch] for c in range(n_chunk)], axis=1)
    l_hi, l_mid, l_lo = _split3(lcat)
    tri = tri_ref[...]
    bcat = _dot(tri, l_hi) + _dot(tri, l_mid) + _dot(tri, l_lo)
    for c in range(n_chunk):
        b_s[c * ch:(c + 1) * ch, :] = bcat[:, c * LANES:(c + 1) * LANES]

    ones = ones_ref[...]
    gn = gn_ref[...]
    sub_row = lax.broadcasted_iota(jnp.int32, (sub, LANES), 0)
    ch_row = lax.broadcasted_iota(jnp.int32, (ch, LANES), 0)

    def chunk(c, carry):
        r0 = pl.multiple_of(c * ch, ch)
        b = b_s[pl.ds(r0, ch), :]
        kk = k_s[pl.ds(r0, ch), :]
        v = v_s[pl.ds(r0, ch), :]
        q = q_s[pl.ds(r0, ch), :]
        st = st_s[...]
        v_bf = v.astype(BF16)
        b_last = b_s[pl.ds(r0 + ch - 1, 1), :]

        o = _dot_nt((q * jnp.exp(b)).astype(BF16), st.astype(BF16))

        k_dec = (kk * jnp.exp(b_last - b)).astype(BF16)
        st_s[...] = jnp.exp(b_last) * st + _dot_tn(v_bf, k_dec)

        a_rows = [jnp.zeros((sub, ch), F32)]
        for i in range(1, n_sub):
            ref = b_s[pl.ds(r0 + sub * i - 1, 1), :]
            qt = q[sub * i:sub * (i + 1)] * jnp.exp(b[sub * i:sub * (i + 1)] - ref)
            kt = jnp.where(ch_row < sub * i, kk * jnp.exp(jnp.minimum(ref - b, 0.0)), 0.0)
            a_rows.append(_dot_nt(qt.astype(BF16), kt.astype(BF16)))
        a_off = jnp.concatenate(a_rows, axis=0)
        o = o + _dot(a_off.astype(BF16), v_bf)

        xs = []
        for i in range(n_sub):
            bi = b[sub * i:sub * (i + 1)]
            qi = q[sub * i:sub * (i + 1)]
            for s in range(sub):
                r = r0 + sub * i + s
                bs = b_s[pl.ds(r, 1), :]
                ks = k_s[pl.ds(r, 1), :]
                e = jnp.exp(jnp.where(sub_row >= s, bi - bs, -jnp.inf))
                xs.append((qi * (ks * e)).astype(BF16))
        y = _dot(jnp.concatenate(xs, axis=0), ones)
        o_diag = []
        for i in range(n_sub):
            acc = jnp.zeros((sub, LANES), F32)
            for s in range(sub):
                n = sub * i + s
                acc = acc + y[n * sub:(n + 1) * sub] * v_s[pl.ds(r0 + n, 1), :]
            o_diag.append(acc)
        o = o + jnp.concatenate(o_diag, axis=0)

        g = g_ref[0, pl.ds(r0, ch), :].astype(F32)
        o_ref[0, pl.ds(r0, ch), :] = (_rms(o, gn) * (g * jax.nn.sigmoid(g))).astype(o_ref.dtype)
        return carry

    lax.fori_loop(0, n_chunk, chunk, 0)


def _hgrn_recurrence(proj, f, gnorm, bsz, seq, d):
    blk = HGRN_BLOCK
    nh = d // HGRN_HEAD_DIM
    proj3 = proj.reshape(bsz, seq, 4 * d)
    f3 = f.reshape(bsz, seq, d)
    tri = jnp.asarray(np.tril(np.ones((HGRN_CHUNK, HGRN_CHUNK), np.float32)), BF16)
    ones = jnp.ones((LANES, LANES), BF16)
    const = lambda b, h, t: (0, 0)
    tok = lambda col: (lambda b, h, t: (b, t, col * nh + h))
    return pl.pallas_call(
        functools.partial(_hgrn_kernel, blk=blk),
        grid=(bsz, nh, seq // blk),
        in_specs=[
            pl.BlockSpec((1, blk, LANES), tok(0)),
            pl.BlockSpec((1, blk, LANES), tok(0)),
            pl.BlockSpec((1, blk, LANES), tok(2)),
            pl.BlockSpec((1, blk, LANES), tok(3)),
            pl.BlockSpec((1, LANES), const),
            pl.BlockSpec((HGRN_CHUNK, HGRN_CHUNK), const),
            pl.BlockSpec((LANES, LANES), const),
        ],
        out_specs=pl.BlockSpec((1, blk, LANES), tok(0)),
        out_shape=jax.ShapeDtypeStruct((bsz, seq, d), BF16),
        scratch_shapes=[pltpu.VMEM((LANES, LANES), F32)] + [pltpu.VMEM((blk, LANES), F32)] * 4,
        compiler_params=pltpu.CompilerParams(
            dimension_semantics=("arbitrary", "arbitrary", "arbitrary"),
            vmem_limit_bytes=VMEM_LIMIT),
        name="hgrn_recurrence",
    )(proj3, f3, proj3, proj3, gnorm, tri, ones).reshape(bsz * seq, d)


def kernel(x, fox_w_in, fox_b_f, fox_w_out, hgrn_w_in, hgrn_lb_table, hgrn_gnorm, hgrn_w_out,
           ffn_w_in, ffn_w_out, norm_gains):
    bsz, seq, d = x.shape
    depth = norm_gains.shape[0]
    assert d == FOX_HEADS * FOX_HEAD_DIM == HGRN_HEADS * HGRN_HEAD_DIM
    assert seq % ROW_TILE == 0 and seq % (2 * ATTN_BLOCK) == 0 and seq % HGRN_BLOCK == 0
    assert ffn_w_out.shape[1] % FFN_COLS == 0
    gains = norm_gains.astype(F32).reshape(depth, 4, 1, d)
    h = x.reshape(bsz * seq, d).astype(F32)

    for layer in range(depth):
        j = layer // 2
        if layer % 2 == 0:
            w_in = fox_w_in[j]
            w_f = jnp.pad(w_in[:, 3 * d:].astype(F32), ((0, 0), (0, LANES - FOX_HEADS)))
            wf_hi = w_f.astype(BF16)
            wf_lo = (w_f - wf_hi.astype(F32)).astype(BF16)
            b_f = jnp.pad(fox_b_f[j].astype(F32), (0, LANES - FOX_HEADS)).reshape(1, LANES)
            qk, vt, aug = _fox_in_proj(h, gains[layer, 0], w_in[:, :2 * d].astype(BF16),
                                       w_in[:, 2 * d:3 * d].T.astype(BF16), wf_hi, wf_lo, b_f, seq)
            mixed = _fox_attention(qk, vt, aug, bsz, seq, d)
            w_out = fox_w_out[j]
        else:
            proj, f = _hgrn_in_proj(h, gains[layer, 0], hgrn_w_in[j].astype(BF16),
                                    hgrn_lb_table.astype(F32), layer)
            mixed = _hgrn_recurrence(proj, f, hgrn_gnorm[j].astype(F32).reshape(1, -1),
                                     bsz, seq, d)
            w_out = hgrn_w_out[j]
        h, hn = _out_proj(mixed, w_out.astype(BF16), h, gains[layer, 1], gains[layer, 2])
        h = _ffn(hn, ffn_w_in[layer].astype(BF16), ffn_w_out[layer].astype(BF16), h,
                 gains[layer, 3])
    return h.reshape(bsz, seq, d).astype(x.dtype)
```

```python
import functools

import numpy as np
import jax
import jax.numpy as jnp
from jax import lax
from jax.experimental import pallas as pl
from jax.experimental.pallas import tpu as pltpu

F32 = jnp.float32
BF16 = jnp.bfloat16

EPS = 1e-6
LANES = 128
FOX_HEADS = 16
FOX_HEAD_DIM = 64
HGRN_HEADS = 8
HGRN_HEAD_DIM = 128
HGRN_CHUNK = 64
HGRN_SUB = 16
HGRN_TILE = 8
AUG = 8
ONES_ROWS = 16
LOG2E = 1.4426950408889634

ROW_TILE = 512
ATTN_BLOCK = 512
HGRN_BLOCK = 1024
HGRN_HEADS_PER_STEP = 4
FFN_COLS = 256
VMEM_LIMIT = 56 * 1024 * 1024


def _dot(a, b):
    return jnp.dot(a, b, preferred_element_type=F32)


def _dot_nt(a, b):
    return lax.dot_general(a, b, (((1,), (1,)), ((), ())), preferred_element_type=F32)


def _dot_tn(a, b):
    return lax.dot_general(a, b, (((0,), (0,)), ((), ())), preferred_element_type=F32)


def _split3(x):
    hi = x.astype(BF16)
    r = x - hi.astype(F32)
    mid = r.astype(BF16)
    lo = (r - mid.astype(F32)).astype(BF16)
    return hi, mid, lo


def _rms(x, g):
    ms = jnp.mean(x * x, axis=-1, keepdims=True)
    return x * lax.rsqrt(ms + EPS) * g


def _fox_in_kernel(x_ref, g_ref, w_ref, wvt_ref, wfh_ref, wfl_ref, bf_ref, tri_ref, psel_ref,
                   pone_ref, qk_ref, vt_ref, aug_ref, carry_s, *, tiles_per_seq, q_scale):
    d = x_ref.shape[1]

    @pl.when(pl.program_id(0) % tiles_per_seq == 0)
    def _():
        carry_s[...] = jnp.zeros_like(carry_s)

    hn = _rms(x_ref[...], g_ref[...])
    hn_hi = hn.astype(BF16)
    hn_lo = (hn - hn_hi.astype(F32)).astype(BF16)
    wfh = wfh_ref[...]
    z = _dot(hn_hi, wfh) + _dot(hn_lo, wfh) + _dot(hn_hi, wfl_ref[...]) + bf_ref[...]
    qk_ref[:, :d] = (_dot(hn_hi, w_ref[:, :d]) * q_scale).astype(qk_ref.dtype)

    logf = jnp.minimum(z, 0.0) - jnp.log1p(jnp.exp(-jnp.abs(z)))
    tri = tri_ref[...]
    l_hi, l_mid, l_lo = _split3(logf)
    c = _dot(tri, l_hi) + _dot(tri, l_mid) + _dot(tri, l_lo) + carry_s[...]
    carry_s[...] = c[c.shape[0] - 1:, :]
    qk_ref[:, d:] = _dot(hn_hi, w_ref[:, d:]).astype(qk_ref.dtype)

    c_hi, c_mid, c_lo = _split3(c * LOG2E)
    ccat = jnp.concatenate([c_hi, c_mid, c_lo], axis=1)
    aug_ref[...] = (_dot(ccat, psel_ref[...]) + pone_ref[...]).astype(aug_ref.dtype)
    vt_ref[...] = _dot_nt(wvt_ref[...], hn_hi).astype(vt_ref.dtype)


def _aug_tables():
    sel = np.zeros((3 * LANES, 2 * LANES), np.float32)
    one = np.zeros((1, 2 * LANES), np.float32)
    for h in range(FOX_HEADS):
        for part in range(3):
            sel[part * LANES + h, AUG * h + part] = 1.0
            sel[part * LANES + h, LANES + AUG * h + 3 + part] = -1.0
            one[0, AUG * h + 3 + part] = 1.0
            one[0, LANES + AUG * h + part] = 1.0
    return jnp.asarray(sel, BF16), jnp.asarray(one, F32)


def _fox_in_proj(h, gain, w_qk, w_vt, wf_hi, wf_lo, b_f, seq):
    m, d = h.shape
    tm = ROW_TILE
    tri = jnp.asarray(np.tril(np.ones((tm, tm), np.float32)), BF16)
    psel, pone = _aug_tables()
    const = lambda i: (0, 0)
    row = lambda i: (i, 0)
    resident = pl.Buffered(1)
    return pl.pallas_call(
        functools.partial(_fox_in_kernel, tiles_per_seq=seq // tm,
                          q_scale=FOX_HEAD_DIM ** -0.5 * LOG2E),
        grid=(m // tm,),
        in_specs=[
            pl.BlockSpec((tm, d), row),
            pl.BlockSpec((1, d), const),
            pl.BlockSpec((d, 2 * d), const, pipeline_mode=resident),
            pl.BlockSpec((d, d), const, pipeline_mode=resident),
            pl.BlockSpec((d, LANES), const),
            pl.BlockSpec((d, LANES), const),
            pl.BlockSpec((1, LANES), const),
            pl.BlockSpec((tm, tm), const),
            pl.BlockSpec((3 * LANES, 2 * LANES), const),
            pl.BlockSpec((1, 2 * LANES), const),
        ],
        out_specs=[
            pl.BlockSpec((tm, 2 * d), row),
            pl.BlockSpec((d, tm), lambda i: (0, i)),
            pl.BlockSpec((tm, 2 * LANES), row),
        ],
        out_shape=[
            jax.ShapeDtypeStruct((m, 2 * d), BF16),
            jax.ShapeDtypeStruct((d, m), BF16),
            jax.ShapeDtypeStruct((m, 2 * LANES), BF16),
        ],
        scratch_shapes=[pltpu.VMEM((1, LANES), F32)],
        compiler_params=pltpu.CompilerParams(
            dimension_semantics=("arbitrary",), vmem_limit_bytes=VMEM_LIMIT),
        name="fox_in_proj",
    )(h, gain, w_qk, w_vt, wf_hi, wf_lo, b_f, tri, psel, pone)


def _fox_attn_kernel(q_ref, augq_ref, k_ref, augk_ref, vt_ref, o_ref,
                     qcat_s, s0_s, s1_s, mx0_s, mx1_s, m_s, acc_s, *, blk):
    hp = pl.program_id(1)
    g = pl.program_id(2)
    nq = 2 * blk
    hd = FOX_HEAD_DIM
    lane = lax.broadcasted_iota(jnp.int32, (nq, LANES), 1)
    q2 = q_ref[0]
    aq = augq_ref[0]
    zero = jnp.zeros_like(q2)
    for half in range(2):
        qm = jnp.where(lane // hd == half, q2, zero)
        am = jnp.where(lane // AUG == 2 * hp + half, aq, zero)
        qcat_s[half] = jnp.concatenate([qm, am], axis=1)
    m_s[...] = jnp.full(m_s.shape, -jnp.inf, F32)
    acc_s[...] = jnp.zeros(acc_s.shape, F32)
    ones_rows = jnp.ones((ONES_ROWS, blk), BF16)

    def scores(kb, s_ref, mx_ref, col0, ncol):
        k0 = pl.multiple_of(kb * blk, blk)
        kcat = jnp.concatenate([k_ref[0, pl.ds(k0, blk), :], augk_ref[0, pl.ds(k0, blk), :]], axis=1)
        for half in range(2):
            st = _dot_nt(kcat, qcat_s[half, col0:col0 + ncol, :])
            s_ref[half, :, col0:col0 + ncol] = st
            if mx_ref is not None:
                mx_ref[half] = jnp.max(st, axis=0, keepdims=True)

    def update(kb, s_ref, mx_ref, col0, ncol):
        k0 = pl.multiple_of(kb * blk, blk)
        for half in range(2):
            st = s_ref[half, :, col0:col0 + ncol]
            if mx_ref is None:
                krow = lax.broadcasted_iota(jnp.int32, st.shape, 0)
                qcol = lax.broadcasted_iota(jnp.int32, st.shape, 1)
                st = jnp.where(krow <= qcol, st, -jnp.inf)
                bmax = jnp.max(st, axis=0, keepdims=True)
            else:
                bmax = mx_ref[half]
            m_prev = m_s[half, :, col0:col0 + ncol]
            m_new = jnp.maximum(m_prev, bmax)
            alpha = jnp.exp2(m_prev - m_new)
            pt = jnp.exp2(st - m_new).astype(BF16)
            vaug = jnp.concatenate([vt_ref[hd * half:hd * (half + 1), pl.ds(k0, blk)], ones_rows],
                                   axis=0)
            acc_s[half, :, col0:col0 + ncol] = (alpha * acc_s[half, :, col0:col0 + ncol]
                                                + _dot(vaug, pt))
            m_s[half, :, col0:col0 + ncol] = m_new

    scores(0, s0_s, mx0_s, 0, nq)

    def pair(t, carry):
        scores(2 * t + 1, s1_s, mx1_s, 0, nq)
        update(2 * t, s0_s, mx0_s, 0, nq)
        scores(2 * t + 2, s0_s, mx0_s, 0, nq)
        update(2 * t + 1, s1_s, mx1_s, 0, nq)
        return carry

    lax.fori_loop(0, g, pair, 0)
    scores(2 * g + 1, s1_s, None, blk, blk)
    update(2 * g, s0_s, None, 0, nq)
    update(2 * g + 1, s1_s, None, blk, blk)

    outs = []
    for half in range(2):
        acc = acc_s[half]
        outs.append(acc[:hd] / acc[hd:hd + 1])
    o_ref[0] = jnp.concatenate(outs, axis=0).T.astype(o_ref.dtype)


def _fox_attention(qk, vt, aug, bsz, seq, d):
    blk = ATTN_BLOCK
    nq = 2 * blk
    n_pair = d // LANES
    qk3 = qk.reshape(bsz, seq, 2 * d)
    aug3 = aug.reshape(bsz, seq, 2 * LANES)
    return pl.pallas_call(
        functools.partial(_fox_attn_kernel, blk=blk),
        grid=(bsz, n_pair, seq // nq),
        in_specs=[
            pl.BlockSpec((1, nq, LANES), lambda b, hp, g: (b, g, hp)),
            pl.BlockSpec((1, nq, LANES), lambda b, hp, g: (b, g, 0)),
            pl.BlockSpec((1, seq, LANES), lambda b, hp, g: (b, 0, n_pair + hp)),
            pl.BlockSpec((1, seq, LANES), lambda b, hp, g: (b, 0, 1)),
            pl.BlockSpec((LANES, seq), lambda b, hp, g: (hp, b)),
        ],
        out_specs=pl.BlockSpec((1, nq, LANES), lambda b, hp, g: (b, g, hp)),
        out_shape=jax.ShapeDtypeStruct((bsz, seq, d), BF16),
        scratch_shapes=[
            pltpu.VMEM((2, nq, 2 * LANES), BF16),
            pltpu.VMEM((2, blk, nq), F32),
            pltpu.VMEM((2, blk, nq), F32),
            pltpu.VMEM((2, 1, nq), F32),
            pltpu.VMEM((2, 1, nq), F32),
            pltpu.VMEM((2, 1, nq), F32),
            pltpu.VMEM((2, FOX_HEAD_DIM + ONES_ROWS, nq), F32),
        ],
        compiler_params=pltpu.CompilerParams(
            dimension_semantics=("arbitrary", "arbitrary", "arbitrary"),
            vmem_limit_bytes=VMEM_LIMIT),
        name="fox_attention",
    )(qk3, aug3, qk3, aug3, vt).reshape(bsz * seq, d)


def _out_proj_kernel(a_ref, w_ref, h_ref, g_post_ref, g_pre_ref, h_out_ref, hn_out_ref):
    y = _dot(a_ref[...], w_ref[...])
    h1 = h_ref[...] + _rms(y, g_post_ref[...])
    h_out_ref[...] = h1
    hn_out_ref[...] = _rms(h1, g_pre_ref[...]).astype(hn_out_ref.dtype)


def _out_proj(a, w, h, g_post, g_pre):
    m, d = h.shape
    tm = ROW_TILE
    const = lambda i: (0, 0)
    row = lambda i: (i, 0)
    return pl.pallas_call(
        _out_proj_kernel,
        grid=(m // tm,),
        in_specs=[
            pl.BlockSpec((tm, d), row),
            pl.BlockSpec((d, d), const),
            pl.BlockSpec((tm, d), row),
            pl.BlockSpec((1, d), const),
            pl.BlockSpec((1, d), const),
        ],
        out_specs=[pl.BlockSpec((tm, d), row), pl.BlockSpec((tm, d), row)],
        out_shape=[jax.ShapeDtypeStruct((m, d), F32), jax.ShapeDtypeStruct((m, d), BF16)],
        compiler_params=pltpu.CompilerParams(
            dimension_semantics=("arbitrary",), vmem_limit_bytes=VMEM_LIMIT),
        name="out_proj",
    )(a, w, h, g_post, g_pre)


def _ffn_kernel(x_ref, wg_ref, wu_ref, wo_ref, h_ref, g_ref, out_ref, *, d_ff):
    x = x_ref[...]
    acc = None
    for c0 in range(0, d_ff, FFN_COLS):
        gate = _dot(x, wg_ref[:, c0:c0 + FFN_COLS])
        up = _dot(x, wu_ref[:, c0:c0 + FFN_COLS])
        act = (gate * jax.nn.sigmoid(gate) * up).astype(BF16)
        part = _dot(act, wo_ref[c0:c0 + FFN_COLS, :])
        acc = part if acc is None else acc + part
    out_ref[...] = h_ref[...] + _rms(acc, g_ref[...])


def _ffn(hn, w_in, w_out, h, gain):
    m, d = h.shape
    d_ff = w_out.shape[0]
    tm = ROW_TILE
    row = lambda i: (i, 0)
    resident = pl.Buffered(1)
    return pl.pallas_call(
        functools.partial(_ffn_kernel, d_ff=d_ff),
        grid=(m // tm,),
        in_specs=[
            pl.BlockSpec((tm, d), row),
            pl.BlockSpec((d, d_ff), lambda i: (0, 0), pipeline_mode=resident),
            pl.BlockSpec((d, d_ff), lambda i: (0, 1), pipeline_mode=resident),
            pl.BlockSpec((d_ff, d), lambda i: (0, 0), pipeline_mode=resident),
            pl.BlockSpec((tm, d), row),
            pl.BlockSpec((1, d), lambda i: (0, 0)),
        ],
        out_specs=pl.BlockSpec((tm, d), row),
        out_shape=jax.ShapeDtypeStruct((m, d), F32),
        compiler_params=pltpu.CompilerParams(
            dimension_semantics=("arbitrary",), vmem_limit_bytes=VMEM_LIMIT),
        name="ffn",
    )(hn, w_in, w_in, w_out, h, gain)


def _hgrn_in_kernel(x_ref, g_ref, w_ref, tab_ref, proj_ref, lf_ref, lk_ref, *, layer):
    d = x_ref.shape[1]
    hn = _rms(x_ref[...], g_ref[...]).astype(BF16)
    z = _dot(hn, w_ref[:, d:2 * d])
    tab = tab_ref[...]
    e = jnp.exp(tab - jnp.max(tab, axis=0, keepdims=True))
    lb = jnp.sum(e[:layer + 1], axis=0, keepdims=True) / jnp.sum(e, axis=0, keepdims=True)
    f = lb + (1.0 - lb) * jax.nn.sigmoid(z)
    lf_ref[...] = jnp.log2(f)
    lk_ref[...] = jnp.log2(1.0 - f)
    for dst, src in enumerate((0, 2, 3)):
        proj_ref[:, dst * d:(dst + 1) * d] = _dot(hn, w_ref[:, src * d:(src + 1) * d]).astype(
            proj_ref.dtype)


def _hgrn_in_proj(h, gain, w, table, layer):
    m, d = h.shape
    tm = ROW_TILE
    const = lambda i: (0, 0)
    row = lambda i: (i, 0)
    return pl.pallas_call(
        functools.partial(_hgrn_in_kernel, layer=layer),
        grid=(m // tm,),
        in_specs=[
            pl.BlockSpec((tm, d), row),
            pl.BlockSpec((1, d), const),
            pl.BlockSpec(w.shape, const, pipeline_mode=pl.Buffered(1)),
            pl.BlockSpec(table.shape, const),
        ],
        out_specs=[pl.BlockSpec((tm, 3 * d), row), pl.BlockSpec((tm, d), row),
                   pl.BlockSpec((tm, d), row)],
        out_shape=[
            jax.ShapeDtypeStruct((m, 3 * d), BF16),
            jax.ShapeDtypeStruct((m, d), F32),
            jax.ShapeDtypeStruct((m, d), F32),
        ],
        compiler_params=pltpu.CompilerParams(
            dimension_semantics=("arbitrary",), vmem_limit_bytes=VMEM_LIMIT),
        name="hgrn_in_proj",
    )(h, gain, w, table)


def _hgrn_kernel(q_ref, lf_ref, lk_ref, i_ref, g_ref, gn_ref, tri_ref, ones_ref, o_ref,
                 st_s, b_s, bk_s, v_s, q_s, oi_s, ao_s, y_s, *, blk, heads):
    ch, sub = HGRN_CHUNK, HGRN_SUB
    tile = HGRN_TILE
    n_chunk = blk // ch
    n_sub = ch // sub
    width = heads * LANES

    @pl.when(pl.program_id(2) == 0)
    def _():
        st_s[...] = jnp.zeros_like(st_s)

    lcat = jnp.concatenate([lf_ref[0, c * ch:(c + 1) * ch, :] for c in range(n_chunk)], axis=1)
    l_hi, l_mid, l_lo = _split3(lcat)
    tri = tri_ref[...]
    bcat = _dot(tri, l_hi) + _dot(tri, l_mid) + _dot(tri, l_lo)
    logk = lk_ref[0]
    for hh in range(heads):
        ln = slice(hh * LANES, (hh + 1) * LANES)
        v_s[hh] = i_ref[0, :, ln].astype(F32)
        q_s[hh] = q_ref[0, :, ln].astype(F32)
        for c in range(n_chunk):
            bc = bcat[:, c * width + hh * LANES:c * width + (hh + 1) * LANES]
            b_s[hh, c * ch:(c + 1) * ch, :] = bc
            bk_s[hh, c * ch:(c + 1) * ch, :] = bc - logk[c * ch:(c + 1) * ch, ln]

    ones = ones_ref[...]
    gn = gn_ref[...]
    row = lax.broadcasted_iota(jnp.int32, (tile, LANES), 0)
    zero_tile = jnp.zeros((tile, LANES), F32)

    def rows(ref, hh, r, n):
        return ref[hh, pl.ds(r, n), :]

    def first_products(r0, hh):
        st = st_s[hh]
        v_bf = rows(v_s, hh, r0, ch).astype(BF16)
        b_last = rows(b_s, hh, r0 + ch - 1, 1)
        o_inter = _dot_nt((rows(q_s, hh, r0, ch) * jnp.exp2(rows(b_s, hh, r0, ch))).astype(BF16),
                          st.astype(BF16))
        k_dec = jnp.exp2(b_last - rows(bk_s, hh, r0, ch)).astype(BF16)
        st_new = jnp.exp2(b_last) * st + _dot_tn(v_bf, k_dec)

        a_rows = [jnp.zeros((sub, ch), F32)]
        for i in range(1, n_sub):
            ref = rows(b_s, hh, r0 + sub * i - 1, 1)
            qt = rows(q_s, hh, r0 + sub * i, sub) * jnp.exp2(rows(b_s, hh, r0 + sub * i, sub) - ref)
            kt = jnp.concatenate([jnp.exp2(ref - rows(bk_s, hh, r0, sub * i)),
                                  jnp.zeros((ch - sub * i, LANES), F32)], axis=0)
            a_rows.append(_dot_nt(qt.astype(BF16), kt.astype(BF16)))

        xs = []
        for i in range(n_sub):
            t0 = r0 + sub * i
            b_lo, b_hi = rows(b_s, hh, t0, tile), rows(b_s, hh, t0 + tile, tile)
            q_lo, q_hi = rows(q_s, hh, t0, tile), rows(q_s, hh, t0 + tile, tile)
            for s in range(sub):
                bks = rows(bk_s, hh, t0 + s, 1)
                if s < tile:
                    xs.append(q_lo * jnp.exp2(jnp.where(row >= s, b_lo - bks, -jnp.inf)))
                    xs.append(q_hi * jnp.exp2(b_hi - bks))
                else:
                    xs.append(q_hi * jnp.exp2(jnp.where(row >= s - tile, b_hi - bks, -jnp.inf)))
        y = _dot(jnp.concatenate(xs, axis=0).astype(BF16), ones)
        st_s[hh] = st_new
        oi_s[hh] = o_inter
        ao_s[hh] = jnp.concatenate(a_rows, axis=0)
        y_s[hh] = y

    def second_products(r0, hh):
        v_bf = rows(v_s, hh, r0, ch).astype(BF16)
        o = oi_s[hh] + _dot(ao_s[hh].astype(BF16), v_bf)
        o_diag = []
        n = 0
        for i in range(n_sub):
            acc_lo, acc_hi = zero_tile, zero_tile
            for s in range(sub):
                vs = rows(v_s, hh, r0 + sub * i + s, 1)
                if s < tile:
                    acc_lo = acc_lo + y_s[hh, n * tile:(n + 1) * tile, :] * vs
                    n += 1
                acc_hi = acc_hi + y_s[hh, n * tile:(n + 1) * tile, :] * vs
                n += 1
            o_diag += [acc_lo, acc_hi]
        o = o + jnp.concatenate(o_diag, axis=0)
        ln = slice(hh * LANES, (hh + 1) * LANES)
        g = g_ref[0, pl.ds(r0, ch), ln].astype(F32)
        o_ref[0, pl.ds(r0, ch), ln] = (_rms(o, gn) * (g * jax.nn.sigmoid(g))).astype(o_ref.dtype)

    for hh in range(heads):
        first_products(0, hh)

    def chunk(c, carry):
        r0 = pl.multiple_of(c * ch, ch)
        r_prev = pl.multiple_of((c - 1) * ch, ch)
        for hh in range(heads):
            second_products(r_prev, hh)
        for hh in range(heads):
            first_products(r0, hh)
        return carry

    lax.fori_loop(1, n_chunk, chunk, 0)
    for hh in range(heads):
        second_products(blk - ch, hh)


def _hgrn_recurrence(proj, lf, lk, gnorm, bsz, seq, d):
    blk = HGRN_BLOCK
    heads = HGRN_HEADS_PER_STEP
    groups = d // (heads * LANES)
    width = heads * LANES
    tiles_per_sub = HGRN_SUB + HGRN_SUB // 2
    diag_rows = (HGRN_CHUNK // HGRN_SUB) * tiles_per_sub * HGRN_TILE
    proj3 = proj.reshape(bsz, seq, 3 * d)
    lf3 = lf.reshape(bsz, seq, d)
    lk3 = lk.reshape(bsz, seq, d)
    tri = jnp.asarray(np.tril(np.ones((HGRN_CHUNK, HGRN_CHUNK), np.float32)), BF16)
    ones = jnp.ones((LANES, LANES), BF16)
    const = lambda b, h, t: (0, 0)
    tok = lambda col: (lambda b, h, t: (b, t, col * groups + h))
    return pl.pallas_call(
        functools.partial(_hgrn_kernel, blk=blk, heads=heads),
        grid=(bsz, groups, seq // blk),
        in_specs=[
            pl.BlockSpec((1, blk, width), tok(0)),
            pl.BlockSpec((1, blk, width), tok(0)),
            pl.BlockSpec((1, blk, width), tok(0)),
            pl.BlockSpec((1, blk, width), tok(1)),
            pl.BlockSpec((1, blk, width), tok(2)),
            pl.BlockSpec((1, LANES), const),
            pl.BlockSpec((HGRN_CHUNK, HGRN_CHUNK), const),
            pl.BlockSpec((LANES, LANES), const),
        ],
        out_specs=pl.BlockSpec((1, blk, width), tok(0)),
        out_shape=jax.ShapeDtypeStruct((bsz, seq, d), BF16),
        scratch_shapes=[pltpu.VMEM((heads, LANES, LANES), F32)]
        + [pltpu.VMEM((heads, blk, LANES), F32)] * 4
        + [pltpu.VMEM((heads, HGRN_CHUNK, LANES), F32),
           pltpu.VMEM((heads, HGRN_CHUNK, HGRN_CHUNK), F32),
           pltpu.VMEM((heads, diag_rows, LANES), F32)],
        compiler_params=pltpu.CompilerParams(
            dimension_semantics=("arbitrary", "arbitrary", "arbitrary"),
            vmem_limit_bytes=VMEM_LIMIT),
        name="hgrn_recurrence",
    )(proj3, lf3, lk3, proj3, proj3, gnorm, tri, ones).reshape(bsz * seq, d)


def kernel(x, fox_w_in, fox_b_f, fox_w_out, hgrn_w_in, hgrn_lb_table, hgrn_gnorm, hgrn_w_out,
           ffn_w_in, ffn_w_out, norm_gains):
    bsz, seq, d = x.shape
    depth = norm_gains.shape[0]
    assert d == FOX_HEADS * FOX_HEAD_DIM == HGRN_HEADS * HGRN_HEAD_DIM
    assert seq % ROW_TILE == 0 and seq % (2 * ATTN_BLOCK) == 0 and seq % HGRN_BLOCK == 0
    assert ffn_w_out.shape[1] % FFN_COLS == 0
    gains = norm_gains.astype(F32).reshape(depth, 4, 1, d)
    h = x.reshape(bsz * seq, d).astype(F32)

    for layer in range(depth):
        j = layer // 2
        if layer % 2 == 0:
            w_in = fox_w_in[j]
            w_f = jnp.pad(w_in[:, 3 * d:].astype(F32), ((0, 0), (0, LANES - FOX_HEADS)))
            wf_hi = w_f.astype(BF16)
            wf_lo = (w_f - wf_hi.astype(F32)).astype(BF16)
            b_f = jnp.pad(fox_b_f[j].astype(F32), (0, LANES - FOX_HEADS)).reshape(1, LANES)
            qk, vt, aug = _fox_in_proj(h, gains[layer, 0], w_in[:, :2 * d].astype(BF16),
                                       w_in[:, 2 * d:3 * d].T.astype(BF16), wf_hi, wf_lo, b_f, seq)
            mixed = _fox_attention(qk, vt, aug, bsz, seq, d)
            w_out = fox_w_out[j]
        else:
            proj, lf, lk = _hgrn_in_proj(h, gains[layer, 0], hgrn_w_in[j].astype(BF16),
                                         hgrn_lb_table.astype(F32), layer)
            mixed = _hgrn_recurrence(proj, lf, lk, hgrn_gnorm[j].astype(F32).reshape(1, -1),
                                     bsz, seq, d)
            w_out = hgrn_w_out[j]
        h, hn = _out_proj(mixed, w_out.astype(BF16), h, gains[layer, 1], gains[layer, 2])
        h = _ffn(hn, ffn_w_in[layer].astype(BF16), ffn_w_out[layer].astype(BF16), h,
                 gains[layer, 3])
    return h.reshape(bsz, seq, d).astype(x.dtype)
```

```python
import functools

import numpy as np
import jax
import jax.numpy as jnp
from jax import lax
from jax.experimental import pallas as pl
from jax.experimental.pallas import tpu as pltpu

F32 = jnp.float32
BF16 = jnp.bfloat16

EPS = 1e-6
LANES = 128
FOX_HEADS = 16
FOX_HEAD_DIM = 64
HGRN_HEADS = 8
HGRN_HEAD_DIM = 128
HGRN_CHUNK = 64
HGRN_SUB = 16
HGRN_TILE = 8
AUG = 8
ONES_ROWS = 16
LOG2E = 1.4426950408889634

ROW_TILE = 512
ATTN_BLOCK = 512
ATTN_COLS = 256
HGRN_BLOCK = 1024
HGRN_HEADS_PER_STEP = 4
FFN_COLS = 256
FFN_ROW_GROUPS = 2
VMEM_LIMIT = 56 * 1024 * 1024


def _dot(a, b):
    return jnp.dot(a, b, preferred_element_type=F32)


def _dot_nt(a, b):
    return lax.dot_general(a, b, (((1,), (1,)), ((), ())), preferred_element_type=F32)


def _dot_tn(a, b):
    return lax.dot_general(a, b, (((0,), (0,)), ((), ())), preferred_element_type=F32)


def _split3(x):
    hi = x.astype(BF16)
    r = x - hi.astype(F32)
    mid = r.astype(BF16)
    lo = (r - mid.astype(F32)).astype(BF16)
    return hi, mid, lo


def _rms(x, g):
    ms = jnp.mean(x * x, axis=-1, keepdims=True)
    return x * lax.rsqrt(ms + EPS) * g


def _fox_in_kernel(x_ref, g_ref, w_ref, wvt_ref, wfh_ref, wfl_ref, bf_ref, tri_ref, psel_ref,
                   pone_ref, qk_ref, vt_ref, aug_ref, carry_s, *, tiles_per_seq, q_scale):
    d = x_ref.shape[1]

    @pl.when(pl.program_id(0) % tiles_per_seq == 0)
    def _():
        carry_s[...] = jnp.zeros_like(carry_s)

    hn = _rms(x_ref[...], g_ref[...])
    hn_hi = hn.astype(BF16)
    hn_lo = (hn - hn_hi.astype(F32)).astype(BF16)
    wfh = wfh_ref[...]
    z = _dot(hn_hi, wfh) + _dot(hn_lo, wfh) + _dot(hn_hi, wfl_ref[...]) + bf_ref[...]
    qk_ref[:, :d] = (_dot(hn_hi, w_ref[:, :d]) * q_scale).astype(qk_ref.dtype)

    logf = jnp.minimum(z, 0.0) - jnp.log1p(jnp.exp(-jnp.abs(z)))
    tri = tri_ref[...]
    l_hi, l_mid, l_lo = _split3(logf)
    c = _dot(tri, l_hi) + _dot(tri, l_mid) + _dot(tri, l_lo) + carry_s[...]
    carry_s[...] = c[c.shape[0] - 1:, :]
    qk_ref[:, d:] = _dot(hn_hi, w_ref[:, d:]).astype(qk_ref.dtype)

    c_hi, c_mid, c_lo = _split3(c * LOG2E)
    ccat = jnp.concatenate([c_hi, c_mid, c_lo], axis=1)
    aug_ref[...] = (_dot(ccat, psel_ref[...]) + pone_ref[...]).astype(aug_ref.dtype)
    vt_ref[...] = _dot_nt(wvt_ref[...], hn_hi).astype(vt_ref.dtype)


def _aug_tables():
    sel = np.zeros((3 * LANES, 2 * LANES), np.float32)
    one = np.zeros((1, 2 * LANES), np.float32)
    for h in range(FOX_HEADS):
        for part in range(3):
            sel[part * LANES + h, AUG * h + part] = 1.0
            sel[part * LANES + h, LANES + AUG * h + 3 + part] = -1.0
            one[0, AUG * h + 3 + part] = 1.0
            one[0, LANES + AUG * h + part] = 1.0
    return jnp.asarray(sel, BF16), jnp.asarray(one, F32)


def _fox_in_proj(h, gain, w_qk, w_vt, wf_hi, wf_lo, b_f, seq):
    m, d = h.shape
    tm = ROW_TILE
    tri = jnp.asarray(np.tril(np.ones((tm, tm), np.float32)), BF16)
    psel, pone = _aug_tables()
    const = lambda i: (0, 0)
    row = lambda i: (i, 0)
    resident = pl.Buffered(1)
    return pl.pallas_call(
        functools.partial(_fox_in_kernel, tiles_per_seq=seq // tm,
                          q_scale=FOX_HEAD_DIM ** -0.5 * LOG2E),
        grid=(m // tm,),
        in_specs=[
            pl.BlockSpec((tm, d), row),
            pl.BlockSpec((1, d), const),
            pl.BlockSpec((d, 2 * d), const, pipeline_mode=resident),
            pl.BlockSpec((d, d), const, pipeline_mode=resident),
            pl.BlockSpec((d, LANES), const),
            pl.BlockSpec((d, LANES), const),
            pl.BlockSpec((1, LANES), const),
            pl.BlockSpec((tm, tm), const),
            pl.BlockSpec((3 * LANES, 2 * LANES), const),
            pl.BlockSpec((1, 2 * LANES), const),
        ],
        out_specs=[
            pl.BlockSpec((tm, 2 * d), row),
            pl.BlockSpec((d, tm), lambda i: (0, i)),
            pl.BlockSpec((tm, 2 * LANES), row),
        ],
        out_shape=[
            jax.ShapeDtypeStruct((m, 2 * d), BF16),
            jax.ShapeDtypeStruct((d, m), BF16),
            jax.ShapeDtypeStruct((m, 2 * LANES), BF16),
        ],
        scratch_shapes=[pltpu.VMEM((1, LANES), F32)],
        compiler_params=pltpu.CompilerParams(
            dimension_semantics=("arbitrary",), vmem_limit_bytes=VMEM_LIMIT),
        name="fox_in_proj",
    )(h, gain, w_qk, w_vt, wf_hi, wf_lo, b_f, tri, psel, pone)


def _fox_attn_kernel(q_ref, augq_ref, k_ref, augk_ref, vt_ref, o_ref,
                     qcat_s, s0_s, s1_s, mx0_s, mx1_s, m_s, acc_s, *, blk):
    hp = pl.program_id(1)
    g = pl.program_id(2)
    nq = 2 * blk
    hd = FOX_HEAD_DIM
    lane = lax.broadcasted_iota(jnp.int32, (nq, LANES), 1)
    q2 = q_ref[0]
    aq = augq_ref[0]
    zero = jnp.zeros_like(q2)
    for half in range(2):
        qm = jnp.where(lane // hd == half, q2, zero)
        am = jnp.where(lane // AUG == 2 * hp + half, aq, zero)
        qcat_s[half] = jnp.concatenate([qm, am], axis=1)
    m_s[...] = jnp.full(m_s.shape, -jnp.inf, F32)
    acc_s[...] = jnp.zeros(acc_s.shape, F32)
    ones_rows = jnp.ones((ONES_ROWS, blk), BF16)

    col = ATTN_COLS
    n_col = nq // col
    krow = lax.broadcasted_iota(jnp.int32, (blk, col), 0)
    qcol = lax.broadcasted_iota(jnp.int32, (blk, col), 1)

    def key_rows(kb):
        k0 = pl.multiple_of(kb * blk, blk)
        return jnp.concatenate([k_ref[0, pl.ds(k0, blk), :], augk_ref[0, pl.ds(k0, blk), :]], axis=1)

    def value_rows(kb):
        k0 = pl.multiple_of(kb * blk, blk)
        return [jnp.concatenate([vt_ref[hd * half:hd * (half + 1), pl.ds(k0, blk)], ones_rows], axis=0)
                for half in range(2)]

    def score_tile(kcat, s_ref, mx_ref, half, n):
        cols = slice(n * col, (n + 1) * col)
        st = _dot_nt(kcat, qcat_s[half, cols, :])
        s_ref[half, :, cols] = st
        if mx_ref is not None:
            mx_ref[half, :, cols] = jnp.max(st, axis=0, keepdims=True)

    def update_tile(vaug, s_ref, mx_ref, half, n, diag_offset=None):
        cols = slice(n * col, (n + 1) * col)
        st = s_ref[half, :, cols]
        if mx_ref is None:
            if diag_offset < blk - 1:
                st = jnp.where(krow - qcol <= diag_offset, st, -jnp.inf)
            bmax = jnp.max(st, axis=0, keepdims=True)
        else:
            bmax = mx_ref[half, :, cols]
        m_prev = m_s[half, :, cols]
        m_new = jnp.maximum(m_prev, bmax)
        alpha = jnp.exp2(m_prev - m_new)
        pt = jnp.exp2(st - m_new).astype(BF16)
        acc_s[half, :, cols] = alpha * acc_s[half, :, cols] + _dot(vaug[half], pt)
        m_s[half, :, cols] = m_new

    def sweep(kb_scores, s_new, mx_new, kb_update, s_cur, mx_cur):
        kcat = key_rows(kb_scores)
        vaug = value_rows(kb_update)
        for n in range(n_col):
            for half in range(2):
                score_tile(kcat, s_new, mx_new, half, n)
                update_tile(vaug, s_cur, mx_cur, half, n)

    kcat = key_rows(0)
    for half in range(2):
        for n in range(n_col):
            score_tile(kcat, s0_s, mx0_s, half, n)

    def pair(t, carry):
        sweep(2 * t + 1, s1_s, mx1_s, 2 * t, s0_s, mx0_s)
        sweep(2 * t + 2, s0_s, mx0_s, 2 * t + 1, s1_s, mx1_s)
        return carry

    lax.fori_loop(0, g, pair, 0)
    kcat = key_rows(2 * g + 1)
    vaug = value_rows(2 * g)
    for half in range(2):
        for n in range(n_col):
            if n * col >= blk:
                score_tile(kcat, s1_s, None, half, n)
            update_tile(vaug, s0_s, None, half, n, diag_offset=n * col)
    vaug = value_rows(2 * g + 1)
    for half in range(2):
        for n in range(blk // col, n_col):
            update_tile(vaug, s1_s, None, half, n, diag_offset=n * col - blk)

    outs = []
    for half in range(2):
        acc = acc_s[half]
        outs.append(acc[:hd] / acc[hd:hd + 1])
    o_ref[0] = jnp.concatenate(outs, axis=0).T.astype(o_ref.dtype)


def _fox_attention(qk, vt, aug, bsz, seq, d):
    blk = ATTN_BLOCK
    nq = 2 * blk
    n_pair = d // LANES
    qk3 = qk.reshape(bsz, seq, 2 * d)
    aug3 = aug.reshape(bsz, seq, 2 * LANES)
    return pl.pallas_call(
        functools.partial(_fox_attn_kernel, blk=blk),
        grid=(bsz, n_pair, seq // nq),
        in_specs=[
            pl.BlockSpec((1, nq, LANES), lambda b, hp, g: (b, g, hp)),
            pl.BlockSpec((1, nq, LANES), lambda b, hp, g: (b, g, 0)),
            pl.BlockSpec((1, seq, LANES), lambda b, hp, g: (b, 0, n_pair + hp)),
            pl.BlockSpec((1, seq, LANES), lambda b, hp, g: (b, 0, 1)),
            pl.BlockSpec((LANES, seq), lambda b, hp, g: (hp, b)),
        ],
        out_specs=pl.BlockSpec((1, nq, LANES), lambda b, hp, g: (b, g, hp)),
        out_shape=jax.ShapeDtypeStruct((bsz, seq, d), BF16),
        scratch_shapes=[
            pltpu.VMEM((2, nq, 2 * LANES), BF16),
            pltpu.VMEM((2, blk, nq), F32),
            pltpu.VMEM((2, blk, nq), F32),
            pltpu.VMEM((2, 1, nq), F32),
            pltpu.VMEM((2, 1, nq), F32),
            pltpu.VMEM((2, 1, nq), F32),
            pltpu.VMEM((2, FOX_HEAD_DIM + ONES_ROWS, nq), F32),
        ],
        compiler_params=pltpu.CompilerParams(
            dimension_semantics=("arbitrary", "arbitrary", "arbitrary"),
            vmem_limit_bytes=VMEM_LIMIT),
        name="fox_attention",
    )(qk3, aug3, qk3, aug3, vt).reshape(bsz * seq, d)


def _mix_ffn_kernel(a_ref, wm_ref, wg_ref, wu_ref, wo_ref, h_ref, gains_ref, out_ref, *, d_ff):
    tm = a_ref.shape[0]
    rows = tm // FFN_ROW_GROUPS
    g_post, g_pre, g_ffn = gains_ref[0:1, :], gains_ref[1:2, :], gains_ref[2:3, :]
    groups = [slice(r * rows, (r + 1) * rows) for r in range(FFN_ROW_GROUPS)]
    y = [_dot(a_ref[rs, :], wm_ref[...]) for rs in groups]
    h1, acc = [], []
    for r, rs in enumerate(groups):
        h1.append(h_ref[rs, :] + _rms(y[r], g_post))
        x = _rms(h1[r], g_pre).astype(BF16)

        def gate_up(c0):
            return _dot(x, wg_ref[:, c0:c0 + FFN_COLS]), _dot(x, wu_ref[:, c0:c0 + FFN_COLS])

        part_sum = None
        nxt = gate_up(0)
        for c0 in range(0, d_ff, FFN_COLS):
            gate, up = nxt
            if c0 + FFN_COLS < d_ff:
                nxt = gate_up(c0 + FFN_COLS)
            act = (gate * jax.nn.sigmoid(gate) * up).astype(BF16)
            part = _dot(act, wo_ref[c0:c0 + FFN_COLS, :])
            part_sum = part if part_sum is None else part_sum + part
        acc.append(part_sum)
    for r, rs in enumerate(groups):
        out_ref[rs, :] = h1[r] + _rms(acc[r], g_ffn)


def _mix_ffn(a, w_mix, w_in, w_out, h, gains):
    m, d = h.shape
    d_ff = w_out.shape[0]
    tm = ROW_TILE
    row = lambda i: (i, 0)
    const = lambda i: (0, 0)
    resident = pl.Buffered(1)
    return pl.pallas_call(
        functools.partial(_mix_ffn_kernel, d_ff=d_ff),
        grid=(m // tm,),
        in_specs=[
            pl.BlockSpec((tm, d), row),
            pl.BlockSpec((d, d), const, pipeline_mode=resident),
            pl.BlockSpec((d, d_ff), const, pipeline_mode=resident),
            pl.BlockSpec((d, d_ff), lambda i: (0, 1), pipeline_mode=resident),
            pl.BlockSpec((d_ff, d), const, pipeline_mode=resident),
            pl.BlockSpec((tm, d), row),
            pl.BlockSpec((3, d), const),
        ],
        out_specs=pl.BlockSpec((tm, d), row),
        out_shape=jax.ShapeDtypeStruct((m, d), F32),
        compiler_params=pltpu.CompilerParams(
            dimension_semantics=("arbitrary",), vmem_limit_bytes=VMEM_LIMIT),
        name="mix_ffn",
    )(a, w_mix, w_in, w_in, w_out, h, gains)


def _hgrn_in_kernel(x_ref, g_ref, w_ref, tab_ref, proj_ref, lf_ref, lk_ref, *, layer):
    d = x_ref.shape[1]
    hn = _rms(x_ref[...], g_ref[...]).astype(BF16)
    z = _dot(hn, w_ref[:, d:2 * d])
    tab = tab_ref[...]
    e = jnp.exp(tab - jnp.max(tab, axis=0, keepdims=True))
    lb = jnp.sum(e[:layer + 1], axis=0, keepdims=True) / jnp.sum(e, axis=0, keepdims=True)
    f = lb + (1.0 - lb) * jax.nn.sigmoid(z)
    lf_ref[...] = jnp.log2(f)
    lk_ref[...] = jnp.log2(1.0 - f)
    for dst, src in enumerate((0, 2, 3)):
        proj_ref[:, dst * d:(dst + 1) * d] = _dot(hn, w_ref[:, src * d:(src + 1) * d]).astype(
            proj_ref.dtype)


def _hgrn_in_proj(h, gain, w, table, layer):
    m, d = h.shape
    tm = ROW_TILE
    const = lambda i: (0, 0)
    row = lambda i: (i, 0)
    return pl.pallas_call(
        functools.partial(_hgrn_in_kernel, layer=layer),
        grid=(m // tm,),
        in_specs=[
            pl.BlockSpec((tm, d), row),
            pl.BlockSpec((1, d), const),
            pl.BlockSpec(w.shape, const, pipeline_mode=pl.Buffered(1)),
            pl.BlockSpec(table.shape, const),
        ],
        out_specs=[pl.BlockSpec((tm, 3 * d), row), pl.BlockSpec((tm, d), row),
                   pl.BlockSpec((tm, d), row)],
        out_shape=[
            jax.ShapeDtypeStruct((m, 3 * d), BF16),
            jax.ShapeDtypeStruct((m, d), F32),
            jax.ShapeDtypeStruct((m, d), F32),
        ],
        compiler_params=pltpu.CompilerParams(
            dimension_semantics=("arbitrary",), vmem_limit_bytes=VMEM_LIMIT),
        name="hgrn_in_proj",
    )(h, gain, w, table)


def _hgrn_kernel(q_ref, lf_ref, lk_ref, i_ref, g_ref, gn_ref, tri_ref, ones_ref, o_ref,
                 st_s, b_s, bk_s, v_s, q_s, oi_s, ao_s, y_s, *, blk, heads):
    ch, sub = HGRN_CHUNK, HGRN_SUB
    tile = HGRN_TILE
    n_chunk = blk // ch
    n_sub = ch // sub
    width = heads * LANES

    @pl.when(pl.program_id(2) == 0)
    def _():
        st_s[...] = jnp.zeros_like(st_s)

    lcat = jnp.concatenate([lf_ref[0, c * ch:(c + 1) * ch, :] for c in range(n_chunk)], axis=1)
    l_hi, l_mid, l_lo = _split3(lcat)
    tri = tri_ref[...]
    bcat = _dot(tri, l_hi) + _dot(tri, l_mid) + _dot(tri, l_lo)
    logk = lk_ref[0]
    for hh in range(heads):
        ln = slice(hh * LANES, (hh + 1) * LANES)
        v_s[hh] = i_ref[0, :, ln].astype(F32)
        q_s[hh] = q_ref[0, :, ln].astype(F32)
        for c in range(n_chunk):
            bc = bcat[:, c * width + hh * LANES:c * width + (hh + 1) * LANES]
            b_s[hh, c * ch:(c + 1) * ch, :] = bc
            bk_s[hh, c * ch:(c + 1) * ch, :] = bc - logk[c * ch:(c + 1) * ch, ln]

    ones = ones_ref[...]
    gn = gn_ref[...]
    row = lax.broadcasted_iota(jnp.int32, (tile, LANES), 0)
    zero_tile = jnp.zeros((tile, LANES), F32)

    def rows(ref, hh, r, n):
        return ref[hh, pl.ds(r, n), :]

    def first_products(r0, hh):
        st = st_s[hh]
        v_bf = rows(v_s, hh, r0, ch).astype(BF16)
        b_last = rows(b_s, hh, r0 + ch - 1, 1)
        o_inter = _dot_nt((rows(q_s, hh, r0, ch) * jnp.exp2(rows(b_s, hh, r0, ch))).astype(BF16),
                          st.astype(BF16))
        k_dec = jnp.exp2(b_last - rows(bk_s, hh, r0, ch)).astype(BF16)
        st_new = jnp.exp2(b_last) * st + _dot_tn(v_bf, k_dec)

        a_rows = [jnp.zeros((sub, ch), F32)]
        for i in range(1, n_sub):
            ref = rows(b_s, hh, r0 + sub * i - 1, 1)
            qt = rows(q_s, hh, r0 + sub * i, sub) * jnp.exp2(rows(b_s, hh, r0 + sub * i, sub) - ref)
            kt = jnp.concatenate([jnp.exp2(ref - rows(bk_s, hh, r0, sub * i)),
                                  jnp.zeros((ch - sub * i, LANES), F32)], axis=0)
            a_rows.append(_dot_nt(qt.astype(BF16), kt.astype(BF16)))

        xs = []
        for i in range(n_sub):
            t0 = r0 + sub * i
            b_lo, b_hi = rows(b_s, hh, t0, tile), rows(b_s, hh, t0 + tile, tile)
            q_lo, q_hi = rows(q_s, hh, t0, tile), rows(q_s, hh, t0 + tile, tile)
            for s in range(sub):
                bks = rows(bk_s, hh, t0 + s, 1)
                if s < tile:
                    xs.append(q_lo * jnp.exp2(jnp.where(row >= s, b_lo - bks, -jnp.inf)))
                    xs.append(q_hi * jnp.exp2(b_hi - bks))
                else:
                    xs.append(q_hi * jnp.exp2(jnp.where(row >= s - tile, b_hi - bks, -jnp.inf)))
        y = _dot(jnp.concatenate(xs, axis=0).astype(BF16), ones)
        st_s[hh] = st_new
        oi_s[hh] = o_inter
        ao_s[hh] = jnp.concatenate(a_rows, axis=0)
        y_s[hh] = y

    def second_products(r0, hh):
        v_bf = rows(v_s, hh, r0, ch).astype(BF16)
        o = oi_s[hh] + _dot(ao_s[hh].astype(BF16), v_bf)
        o_diag = []
        n = 0
        for i in range(n_sub):
            acc_lo, acc_hi = zero_tile, zero_tile
            for s in range(sub):
                vs = rows(v_s, hh, r0 + sub * i + s, 1)
                if s < tile:
                    acc_lo = acc_lo + y_s[hh, n * tile:(n + 1) * tile, :] * vs
                    n += 1
                acc_hi = acc_hi + y_s[hh, n * tile:(n + 1) * tile, :] * vs
                n += 1
            o_diag += [acc_lo, acc_hi]
        o = o + jnp.concatenate(o_diag, axis=0)
        ln = slice(hh * LANES, (hh + 1) * LANES)
        g = g_ref[0, pl.ds(r0, ch), ln].astype(F32)
        o_ref[0, pl.ds(r0, ch), ln] = (_rms(o, gn) * (g * jax.nn.sigmoid(g))).astype(o_ref.dtype)

    for hh in range(heads):
        first_products(0, hh)

    def chunk(c, carry):
        r0 = pl.multiple_of(c * ch, ch)
        r_prev = pl.multiple_of((c - 1) * ch, ch)
        for hh in range(heads):
            second_products(r_prev, hh)
        for hh in range(heads):
            first_products(r0, hh)
        return carry

    lax.fori_loop(1, n_chunk, chunk, 0)
    for hh in range(heads):
        second_products(blk - ch, hh)


def _hgrn_recurrence(proj, lf, lk, gnorm, bsz, seq, d):
    blk = HGRN_BLOCK
    heads = HGRN_HEADS_PER_STEP
    groups = d // (heads * LANES)
    width = heads * LANES
    tiles_per_sub = HGRN_SUB + HGRN_SUB // 2
    diag_rows = (HGRN_CHUNK // HGRN_SUB) * tiles_per_sub * HGRN_TILE
    proj3 = proj.reshape(bsz, seq, 3 * d)
    lf3 = lf.reshape(bsz, seq, d)
    lk3 = lk.reshape(bsz, seq, d)
    tri = jnp.asarray(np.tril(np.ones((HGRN_CHUNK, HGRN_CHUNK), np.float32)), BF16)
    ones = jnp.ones((LANES, LANES), BF16)
    const = lambda b, h, t: (0, 0)
    tok = lambda col: (lambda b, h, t: (b, t, col * groups + h))
    return pl.pallas_call(
        functools.partial(_hgrn_kernel, blk=blk, heads=heads),
        grid=(bsz, groups, seq // blk),
        in_specs=[
            pl.BlockSpec((1, blk, width), tok(0)),
            pl.BlockSpec((1, blk, width), tok(0)),
            pl.BlockSpec((1, blk, width), tok(0)),
            pl.BlockSpec((1, blk, width), tok(1)),
            pl.BlockSpec((1, blk, width), tok(2)),
            pl.BlockSpec((1, LANES), const),
            pl.BlockSpec((HGRN_CHUNK, HGRN_CHUNK), const),
            pl.BlockSpec((LANES, LANES), const),
        ],
        out_specs=pl.BlockSpec((1, blk, width), tok(0)),
        out_shape=jax.ShapeDtypeStruct((bsz, seq, d), BF16),
        scratch_shapes=[pltpu.VMEM((heads, LANES, LANES), F32)]
        + [pltpu.VMEM((heads, blk, LANES), F32)] * 4
        + [pltpu.VMEM((heads, HGRN_CHUNK, LANES), F32),
           pltpu.VMEM((heads, HGRN_CHUNK, HGRN_CHUNK), F32),
           pltpu.VMEM((heads, diag_rows, LANES), F32)],
        compiler_params=pltpu.CompilerParams(
            dimension_semantics=("arbitrary", "arbitrary", "arbitrary"),
            vmem_limit_bytes=VMEM_LIMIT),
        name="hgrn_recurrence",
    )(proj3, lf3, lk3, proj3, proj3, gnorm, tri, ones).reshape(bsz * seq, d)


def kernel(x, fox_w_in, fox_b_f, fox_w_out, hgrn_w_in, hgrn_lb_table, hgrn_gnorm, hgrn_w_out,
           ffn_w_in, ffn_w_out, norm_gains):
    bsz, seq, d = x.shape
    depth = norm_gains.shape[0]
    assert d == FOX_HEADS * FOX_HEAD_DIM == HGRN_HEADS * HGRN_HEAD_DIM
    assert seq % ROW_TILE == 0 and seq % (2 * ATTN_BLOCK) == 0 and seq % HGRN_BLOCK == 0
    assert ffn_w_out.shape[1] % FFN_COLS == 0
    gains = norm_gains.astype(F32).reshape(depth, 4, 1, d)
    h = x.reshape(bsz * seq, d).astype(F32)

    for layer in range(depth):
        j = layer // 2
        if layer % 2 == 0:
            w_in = fox_w_in[j]
            w_f = jnp.pad(w_in[:, 3 * d:].astype(F32), ((0, 0), (0, LANES - FOX_HEADS)))
            wf_hi = w_f.astype(BF16)
            wf_lo = (w_f - wf_hi.astype(F32)).astype(BF16)
            b_f = jnp.pad(fox_b_f[j].astype(F32), (0, LANES - FOX_HEADS)).reshape(1, LANES)
            qk, vt, aug = _fox_in_proj(h, gains[layer, 0], w_in[:, :2 * d].astype(BF16),
                                       w_in[:, 2 * d:3 * d].T.astype(BF16), wf_hi, wf_lo, b_f, seq)
            mixed = _fox_attention(qk, vt, aug, bsz, seq, d)
            w_out = fox_w_out[j]
        else:
            proj, lf, lk = _hgrn_in_proj(h, gains[layer, 0], hgrn_w_in[j].astype(BF16),
                                         hgrn_lb_table.astype(F32), layer)
            mixed = _hgrn_recurrence(proj, lf, lk, hgrn_gnorm[j].astype(F32).reshape(1, -1),
                                     bsz, seq, d)
            w_out = hgrn_w_out[j]
        h = _mix_ffn(mixed, w_out.astype(BF16), ffn_w_in[layer].astype(BF16),
                     ffn_w_out[layer].astype(BF16), h, norm_gains[layer, 1:].astype(F32))
    return h.reshape(bsz, seq, d).astype(x.dtype)
```

```python
import functools

import numpy as np
import jax
import jax.numpy as jnp
from jax import lax
from jax.experimental import pallas as pl
from jax.experimental.pallas import tpu as pltpu

F32 = jnp.float32
BF16 = jnp.bfloat16

EPS = 1e-6
LANES = 128
FOX_HEADS = 16
FOX_HEAD_DIM = 64
HGRN_HEADS = 8
HGRN_HEAD_DIM = 128
HGRN_CHUNK = 64
HGRN_SUB = 16
HGRN_TILE = 8
AUG = 8
ONES_ROWS = 16
LOG2E = 1.4426950408889634

ROW_TILE = 512
ATTN_BLOCK = 512
ATTN_COLS = 256
HGRN_BLOCK = 512
HGRN_HEADS_PER_STEP = 8
FFN_COLS = 256
FFN_ROW_GROUPS = 2
IN_ROW_GROUPS = 2
VMEM_LIMIT = 56 * 1024 * 1024


def _dot(a, b):
    return jnp.dot(a, b, preferred_element_type=F32)


def _dot_nt(a, b):
    return lax.dot_general(a, b, (((1,), (1,)), ((), ())), preferred_element_type=F32)


def _dot_tn(a, b):
    return lax.dot_general(a, b, (((0,), (0,)), ((), ())), preferred_element_type=F32)


def _split3(x):
    hi = x.astype(BF16)
    r = x - hi.astype(F32)
    mid = r.astype(BF16)
    lo = (r - mid.astype(F32)).astype(BF16)
    return hi, mid, lo


def _rms(x, g):
    ms = jnp.mean(x * x, axis=-1, keepdims=True)
    return x * lax.rsqrt(ms + EPS) * g


def _fox_in_kernel(x_ref, g_ref, w_ref, wvt_ref, wfh_ref, wfl_ref, bf_ref, tri_ref, psel_ref,
                   pone_ref, qk_ref, vt_ref, aug_ref, carry_s, *, tiles_per_seq, q_scale):
    d = x_ref.shape[1]

    @pl.when(pl.program_id(0) % tiles_per_seq == 0)
    def _():
        carry_s[...] = jnp.zeros_like(carry_s)

    hn = _rms(x_ref[...], g_ref[...])
    hn_hi = hn.astype(BF16)
    hn_lo = (hn - hn_hi.astype(F32)).astype(BF16)
    wfh = wfh_ref[...]
    z = _dot(hn_hi, wfh) + _dot(hn_lo, wfh) + _dot(hn_hi, wfl_ref[...]) + bf_ref[...]
    qk_ref[:, :d] = (_dot(hn_hi, w_ref[:, :d]) * q_scale).astype(qk_ref.dtype)

    logf = jnp.minimum(z, 0.0) - jnp.log1p(jnp.exp(-jnp.abs(z)))
    tri = tri_ref[...]
    l_hi, l_mid, l_lo = _split3(logf)
    c = _dot(tri, l_hi) + _dot(tri, l_mid) + _dot(tri, l_lo) + carry_s[...]
    carry_s[...] = c[c.shape[0] - 1:, :]
    qk_ref[:, d:] = _dot(hn_hi, w_ref[:, d:]).astype(qk_ref.dtype)

    c_hi, c_mid, c_lo = _split3(c * LOG2E)
    ccat = jnp.concatenate([c_hi, c_mid, c_lo], axis=1)
    aug_ref[...] = (_dot(ccat, psel_ref[...]) + pone_ref[...]).astype(aug_ref.dtype)
    vt = lax.dot_general(wvt_ref[...], hn_hi, (((0,), (1,)), ((), ())), preferred_element_type=F32)
    vt_ref[...] = vt.astype(vt_ref.dtype)


def _aug_tables():
    sel = np.zeros((3 * LANES, 2 * LANES), np.float32)
    one = np.zeros((1, 2 * LANES), np.float32)
    for h in range(FOX_HEADS):
        for part in range(3):
            sel[part * LANES + h, AUG * h + part] = 1.0
            sel[part * LANES + h, LANES + AUG * h + 3 + part] = -1.0
            one[0, AUG * h + 3 + part] = 1.0
            one[0, LANES + AUG * h + part] = 1.0
    return jnp.asarray(sel, BF16), jnp.asarray(one, F32)


def _fox_in_proj(h, gain, w_qk, w_vt, wf_hi, wf_lo, b_f, seq):
    m, d = h.shape
    tm = ROW_TILE
    tri = jnp.asarray(np.tril(np.ones((tm, tm), np.float32)), BF16)
    psel, pone = _aug_tables()
    const = lambda i: (0, 0)
    row = lambda i: (i, 0)
    resident = pl.Buffered(1)
    return pl.pallas_call(
        functools.partial(_fox_in_kernel, tiles_per_seq=seq // tm,
                          q_scale=FOX_HEAD_DIM ** -0.5 * LOG2E),
        grid=(m // tm,),
        in_specs=[
            pl.BlockSpec((tm, d), row),
            pl.BlockSpec((1, d), const),
            pl.BlockSpec((d, 2 * d), const, pipeline_mode=resident),
            pl.BlockSpec((d, d), lambda i: (0, 2), pipeline_mode=resident),
            pl.BlockSpec((d, LANES), const),
            pl.BlockSpec((d, LANES), const),
            pl.BlockSpec((1, LANES), const),
            pl.BlockSpec((tm, tm), const),
            pl.BlockSpec((3 * LANES, 2 * LANES), const),
            pl.BlockSpec((1, 2 * LANES), const),
        ],
        out_specs=[
            pl.BlockSpec((tm, 2 * d), row),
            pl.BlockSpec((d, tm), lambda i: (0, i)),
            pl.BlockSpec((tm, 2 * LANES), row),
        ],
        out_shape=[
            jax.ShapeDtypeStruct((m, 2 * d), BF16),
            jax.ShapeDtypeStruct((d, m), BF16),
            jax.ShapeDtypeStruct((m, 2 * LANES), BF16),
        ],
        scratch_shapes=[pltpu.VMEM((1, LANES), F32)],
        compiler_params=pltpu.CompilerParams(
            dimension_semantics=("arbitrary",), vmem_limit_bytes=VMEM_LIMIT),
        name="fox_in_proj",
    )(h, gain, w_qk, w_vt, wf_hi, wf_lo, b_f, tri, psel, pone)


def _fox_attn_kernel(q_ref, augq_ref, k_ref, augk_ref, vt_ref, o_ref,
                     qcat_s, s0_s, s1_s, mx0_s, mx1_s, m_s, acc_s, *, blk):
    hp = pl.program_id(1)
    g = pl.program_id(2)
    nq = 2 * blk
    hd = FOX_HEAD_DIM
    lane = lax.broadcasted_iota(jnp.int32, (nq, LANES), 1)
    q2 = q_ref[0]
    aq = augq_ref[0]
    zero = jnp.zeros_like(q2)
    for half in range(2):
        qm = jnp.where(lane // hd == half, q2, zero)
        am = jnp.where(lane // AUG == 2 * hp + half, aq, zero)
        qcat_s[half] = jnp.concatenate([qm, am], axis=1)
    m_s[...] = jnp.full(m_s.shape, -jnp.inf, F32)
    acc_s[...] = jnp.zeros(acc_s.shape, F32)
    ones_rows = jnp.ones((ONES_ROWS, blk), BF16)

    col = ATTN_COLS
    n_col = nq // col
    krow = lax.broadcasted_iota(jnp.int32, (blk, col), 0)
    qcol = lax.broadcasted_iota(jnp.int32, (blk, col), 1)

    def key_rows(kb):
        k0 = pl.multiple_of(kb * blk, blk)
        return jnp.concatenate([k_ref[0, pl.ds(k0, blk), :], augk_ref[0, pl.ds(k0, blk), :]], axis=1)

    def value_rows(kb):
        k0 = pl.multiple_of(kb * blk, blk)
        return [jnp.concatenate([vt_ref[hd * half:hd * (half + 1), pl.ds(k0, blk)], ones_rows], axis=0)
                for half in range(2)]

    def score_tile(kcat, s_ref, mx_ref, half, n):
        cols = slice(n * col, (n + 1) * col)
        st = _dot_nt(kcat, qcat_s[half, cols, :])
        s_ref[half, :, cols] = st
        if mx_ref is not None:
            mx_ref[half, :, cols] = jnp.max(st, axis=0, keepdims=True)

    def update_tile(vaug, s_ref, mx_ref, half, n, diag_offset=None):
        cols = slice(n * col, (n + 1) * col)
        st = s_ref[half, :, cols]
        if mx_ref is None:
            if diag_offset < blk - 1:
                st = jnp.where(krow - qcol <= diag_offset, st, -jnp.inf)
            bmax = jnp.max(st, axis=0, keepdims=True)
        else:
            bmax = mx_ref[half, :, cols]
        m_prev = m_s[half, :, cols]
        m_new = jnp.maximum(m_prev, bmax)
        alpha = jnp.exp2(m_prev - m_new)
        pt = jnp.exp2(st - m_new).astype(BF16)
        acc_s[half, :, cols] = alpha * acc_s[half, :, cols] + _dot(vaug[half], pt)
        m_s[half, :, cols] = m_new

    def sweep(kb_scores, s_new, mx_new, kb_update, s_cur, mx_cur):
        kcat = key_rows(kb_scores)
        vaug = value_rows(kb_update)
        for n in range(n_col):
            for half in range(2):
                score_tile(kcat, s_new, mx_new, half, n)
                update_tile(vaug, s_cur, mx_cur, half, n)

    kcat = key_rows(0)
    for half in range(2):
        for n in range(n_col):
            score_tile(kcat, s0_s, mx0_s, half, n)

    def pair(t, carry):
        sweep(2 * t + 1, s1_s, mx1_s, 2 * t, s0_s, mx0_s)
        sweep(2 * t + 2, s0_s, mx0_s, 2 * t + 1, s1_s, mx1_s)
        return carry

    lax.fori_loop(0, g, pair, 0)
    kcat = key_rows(2 * g + 1)
    vaug = value_rows(2 * g)
    for half in range(2):
        for n in range(n_col):
            if n * col >= blk:
                score_tile(kcat, s1_s, None, half, n)
            update_tile(vaug, s0_s, None, half, n, diag_offset=n * col)
    vaug = value_rows(2 * g + 1)
    for half in range(2):
        for n in range(blk // col, n_col):
            update_tile(vaug, s1_s, None, half, n, diag_offset=n * col - blk)

    outs = []
    for half in range(2):
        acc = acc_s[half]
        outs.append(acc[:hd] / acc[hd:hd + 1])
    o_ref[0] = jnp.concatenate(outs, axis=0).T.astype(o_ref.dtype)


def _fox_attention(qk, vt, aug, bsz, seq, d):
    blk = ATTN_BLOCK
    nq = 2 * blk
    n_pair = d // LANES
    n_step = seq // nq
    qk3 = qk.reshape(bsz, seq, 2 * d)
    aug3 = aug.reshape(bsz, seq, 2 * LANES)
    return pl.pallas_call(
        functools.partial(_fox_attn_kernel, blk=blk),
        grid=(bsz, n_pair, n_step),
        in_specs=[
            pl.BlockSpec((1, nq, LANES), lambda b, hp, g: (b, g, hp)),
            pl.BlockSpec((1, nq, LANES), lambda b, hp, g: (b, g, 0)),
            pl.BlockSpec((1, seq, LANES), lambda b, hp, g: (b, 0, n_pair + hp)),
            pl.BlockSpec((1, seq, LANES), lambda b, hp, g: (b, 0, 1)),
            pl.BlockSpec((LANES, seq), lambda b, hp, g: (hp, b)),
        ],
        out_specs=pl.BlockSpec((1, nq, LANES), lambda b, hp, g: (b, g, hp)),
        out_shape=jax.ShapeDtypeStruct((bsz, seq, d), BF16),
        scratch_shapes=[
            pltpu.VMEM((2, nq, 2 * LANES), BF16),
            pltpu.VMEM((2, blk, nq), F32),
            pltpu.VMEM((2, blk, nq), F32),
            pltpu.VMEM((2, 1, nq), F32),
            pltpu.VMEM((2, 1, nq), F32),
            pltpu.VMEM((2, 1, nq), F32),
            pltpu.VMEM((2, FOX_HEAD_DIM + ONES_ROWS, nq), F32),
        ],
        compiler_params=pltpu.CompilerParams(
            dimension_semantics=("arbitrary", "arbitrary", "arbitrary"),
            vmem_limit_bytes=VMEM_LIMIT),
        name="fox_attention",
    )(qk3, aug3, qk3, aug3, vt).reshape(bsz * seq, d)


def _mix_ffn_kernel(a_ref, wm_ref, wg_ref, wu_ref, wo_ref, h_ref, gains_ref, out_ref, *, d_ff):
    tm = a_ref.shape[0]
    rows = tm // FFN_ROW_GROUPS
    g_post, g_pre, g_ffn = gains_ref[0:1, :], gains_ref[1:2, :], gains_ref[2:3, :]
    groups = [slice(r * rows, (r + 1) * rows) for r in range(FFN_ROW_GROUPS)]
    y = [_dot(a_ref[rs, :], wm_ref[...]) for rs in groups]
    h1, acc = [], []
    for r, rs in enumerate(groups):
        h1.append(h_ref[rs, :] + _rms(y[r], g_post))
        x = _rms(h1[r], g_pre).astype(BF16)

        def gate_up(c0):
            return _dot(x, wg_ref[:, c0:c0 + FFN_COLS]), _dot(x, wu_ref[:, c0:c0 + FFN_COLS])

        part_sum = None
        nxt = gate_up(0)
        for c0 in range(0, d_ff, FFN_COLS):
            gate, up = nxt
            if c0 + FFN_COLS < d_ff:
                nxt = gate_up(c0 + FFN_COLS)
            act = (gate * jax.nn.sigmoid(gate) * up).astype(BF16)
            part = _dot(act, wo_ref[c0:c0 + FFN_COLS, :])
            part_sum = part if part_sum is None else part_sum + part
        acc.append(part_sum)
    for r, rs in enumerate(groups):
        out_ref[rs, :] = h1[r] + _rms(acc[r], g_ffn)


def _mix_ffn(a, w_mix, w_in, w_out, h, gains):
    m, d = h.shape
    d_ff = w_out.shape[0]
    tm = ROW_TILE
    row = lambda i: (i, 0)
    const = lambda i: (0, 0)
    resident = pl.Buffered(1)
    return pl.pallas_call(
        functools.partial(_mix_ffn_kernel, d_ff=d_ff),
        grid=(m // tm,),
        in_specs=[
            pl.BlockSpec((tm, d), row),
            pl.BlockSpec((d, d), const, pipeline_mode=resident),
            pl.BlockSpec((d, d_ff), const, pipeline_mode=resident),
            pl.BlockSpec((d, d_ff), lambda i: (0, 1), pipeline_mode=resident),
            pl.BlockSpec((d_ff, d), const, pipeline_mode=resident),
            pl.BlockSpec((tm, d), row),
            pl.BlockSpec((3, d), const),
        ],
        out_specs=pl.BlockSpec((tm, d), row),
        out_shape=jax.ShapeDtypeStruct((m, d), F32),
        compiler_params=pltpu.CompilerParams(
            dimension_semantics=("arbitrary",), vmem_limit_bytes=VMEM_LIMIT),
        name="mix_ffn",
    )(a, w_mix, w_in, w_in, w_out, h, gains)


def _hgrn_in_kernel(x_ref, g_ref, w_ref, tab_ref, proj_ref, lf_ref, lk_ref, *, layer):
    tm, d = x_ref.shape
    rows = tm // IN_ROW_GROUPS
    tab = tab_ref[...]
    e = jnp.exp(tab - jnp.max(tab, axis=0, keepdims=True))
    lb = jnp.sum(e[:layer + 1], axis=0, keepdims=True) / jnp.sum(e, axis=0, keepdims=True)
    for r in range(IN_ROW_GROUPS):
        rs = slice(r * rows, (r + 1) * rows)
        hn = _rms(x_ref[rs, :], g_ref[...]).astype(BF16)
        z = _dot(hn, w_ref[:, d:2 * d])
        f = lb + (1.0 - lb) * jax.nn.sigmoid(z)
        lf_ref[rs, :] = jnp.log2(f)
        lk_ref[rs, :] = jnp.log2(1.0 - f)
        for dst, src in enumerate((0, 2, 3)):
            proj_ref[rs, dst * d:(dst + 1) * d] = _dot(
                hn, w_ref[:, src * d:(src + 1) * d]).astype(proj_ref.dtype)


def _hgrn_in_proj(h, gain, w, table, layer):
    m, d = h.shape
    tm = ROW_TILE
    const = lambda i: (0, 0)
    row = lambda i: (i, 0)
    return pl.pallas_call(
        functools.partial(_hgrn_in_kernel, layer=layer),
        grid=(m // tm,),
        in_specs=[
            pl.BlockSpec((tm, d), row),
            pl.BlockSpec((1, d), const),
            pl.BlockSpec(w.shape, const, pipeline_mode=pl.Buffered(1)),
            pl.BlockSpec(table.shape, const),
        ],
        out_specs=[pl.BlockSpec((tm, 3 * d), row), pl.BlockSpec((tm, d), row),
                   pl.BlockSpec((tm, d), row)],
        out_shape=[
            jax.ShapeDtypeStruct((m, 3 * d), BF16),
            jax.ShapeDtypeStruct((m, d), F32),
            jax.ShapeDtypeStruct((m, d), F32),
        ],
        compiler_params=pltpu.CompilerParams(
            dimension_semantics=("arbitrary",), vmem_limit_bytes=VMEM_LIMIT),
        name="hgrn_in_proj",
    )(h, gain, w, table)


def _hgrn_kernel(q_ref, lf_ref, lk_ref, i_ref, g_ref, gn_ref, tri_ref, ones_ref, o_ref,
                 st_s, b_s, bk_s, v_s, q_s, oi_s, ao_s, y_s, *, blk, heads):
    ch, sub = HGRN_CHUNK, HGRN_SUB
    tile = HGRN_TILE
    n_chunk = blk // ch
    n_sub = ch // sub
    width = heads * LANES

    @pl.when(pl.program_id(2) == 0)
    def _():
        st_s[...] = jnp.zeros_like(st_s)

    lcat = jnp.concatenate([lf_ref[0, c * ch:(c + 1) * ch, :] for c in range(n_chunk)], axis=1)
    l_hi = lcat.astype(BF16)
    l_lo = (lcat - l_hi.astype(F32)).astype(BF16)
    tri = tri_ref[...]
    bcat = _dot(tri, l_hi) + _dot(tri, l_lo)
    logk = lk_ref[0]
    for hh in range(heads):
        ln = slice(hh * LANES, (hh + 1) * LANES)
        v_s[hh] = i_ref[0, :, ln].astype(F32)
        q_s[hh] = q_ref[0, :, ln].astype(F32)
        for c in range(n_chunk):
            bc = bcat[:, c * width + hh * LANES:c * width + (hh + 1) * LANES]
            b_s[hh, c * ch:(c + 1) * ch, :] = bc
            bk_s[hh, c * ch:(c + 1) * ch, :] = bc - logk[c * ch:(c + 1) * ch, ln]

    ones = ones_ref[...]
    gn = gn_ref[...]
    row = lax.broadcasted_iota(jnp.int32, (tile, LANES), 0)
    zero_tile = jnp.zeros((tile, LANES), F32)

    def rows(ref, hh, r, n):
        return ref[hh, pl.ds(r, n), :]

    def first_products(r0, hh):
        st = st_s[hh]
        v_bf = rows(v_s, hh, r0, ch).astype(BF16)
        b_last = rows(b_s, hh, r0 + ch - 1, 1)
        o_inter = _dot_nt((rows(q_s, hh, r0, ch) * jnp.exp2(rows(b_s, hh, r0, ch))).astype(BF16),
                          st.astype(BF16))
        k_dec = jnp.exp2(b_last - rows(bk_s, hh, r0, ch)).astype(BF16)
        st_new = jnp.exp2(b_last) * st + _dot_tn(v_bf, k_dec)

        a_rows = [jnp.zeros((sub, ch), F32)]
        for i in range(1, n_sub):
            ref = rows(b_s, hh, r0 + sub * i - 1, 1)
            qt = rows(q_s, hh, r0 + sub * i, sub) * jnp.exp2(rows(b_s, hh, r0 + sub * i, sub) - ref)
            kt = jnp.concatenate([jnp.exp2(ref - rows(bk_s, hh, r0, sub * i)),
                                  jnp.zeros((ch - sub * i, LANES), F32)], axis=0)
            a_rows.append(_dot_nt(qt.astype(BF16), kt.astype(BF16)))

        xs = []
        for i in range(n_sub):
            t0 = r0 + sub * i
            b_lo, b_hi = rows(b_s, hh, t0, tile), rows(b_s, hh, t0 + tile, tile)
            q_lo, q_hi = rows(q_s, hh, t0, tile), rows(q_s, hh, t0 + tile, tile)
            for s in range(sub):
                bks = rows(bk_s, hh, t0 + s, 1)
                if s < tile:
                    xs.append(q_lo * jnp.exp2(jnp.where(row >= s, b_lo - bks, -jnp.inf)))
                    xs.append(q_hi * jnp.exp2(b_hi - bks))
                else:
                    xs.append(q_hi * jnp.exp2(jnp.where(row >= s - tile, b_hi - bks, -jnp.inf)))
        y = _dot(jnp.concatenate(xs, axis=0).astype(BF16), ones)
        st_s[hh] = st_new
        oi_s[hh] = o_inter
        ao_s[hh] = jnp.concatenate(a_rows, axis=0)
        y_s[hh] = y

    def second_products(r0, hh):
        v_bf = rows(v_s, hh, r0, ch).astype(BF16)
        o = oi_s[hh] + _dot(ao_s[hh].astype(BF16), v_bf)
        o_diag = []
        n = 0
        for i in range(n_sub):
            acc_lo, acc_hi = zero_tile, zero_tile
            for s in range(sub):
                vs = rows(v_s, hh, r0 + sub * i + s, 1)
                if s < tile:
                    acc_lo = acc_lo + y_s[hh, n * tile:(n + 1) * tile, :] * vs
                    n += 1
                acc_hi = acc_hi + y_s[hh, n * tile:(n + 1) * tile, :] * vs
                n += 1
            o_diag += [acc_lo, acc_hi]
        o = o + jnp.concatenate(o_diag, axis=0)
        ln = slice(hh * LANES, (hh + 1) * LANES)
        g = g_ref[0, pl.ds(r0, ch), ln].astype(F32)
        o_ref[0, pl.ds(r0, ch), ln] = (_rms(o, gn) * (g * jax.nn.sigmoid(g))).astype(o_ref.dtype)

    for hh in range(heads):
        first_products(0, hh)

    def chunk(c, carry):
        r0 = pl.multiple_of(c * ch, ch)
        r_prev = pl.multiple_of((c - 1) * ch, ch)
        for hh in range(heads):
            second_products(r_prev, hh)
        for hh in range(heads):
            first_products(r0, hh)
        return carry

    lax.fori_loop(1, n_chunk, chunk, 0)
    for hh in range(heads):
        second_products(blk - ch, hh)


def _hgrn_recurrence(proj, lf, lk, gnorm, bsz, seq, d):
    blk = HGRN_BLOCK
    heads = HGRN_HEADS_PER_STEP
    groups = d // (heads * LANES)
    width = heads * LANES
    tiles_per_sub = HGRN_SUB + HGRN_SUB // 2
    diag_rows = (HGRN_CHUNK // HGRN_SUB) * tiles_per_sub * HGRN_TILE
    proj3 = proj.reshape(bsz, seq, 3 * d)
    lf3 = lf.reshape(bsz, seq, d)
    lk3 = lk.reshape(bsz, seq, d)
    tri = jnp.asarray(np.tril(np.ones((HGRN_CHUNK, HGRN_CHUNK), np.float32)), BF16)
    ones = jnp.ones((LANES, LANES), BF16)
    const = lambda b, h, t: (0, 0)
    tok = lambda col: (lambda b, h, t: (b, t, col * groups + h))
    return pl.pallas_call(
        functools.partial(_hgrn_kernel, blk=blk, heads=heads),
        grid=(bsz, groups, seq // blk),
        in_specs=[
            pl.BlockSpec((1, blk, width), tok(0)),
            pl.BlockSpec((1, blk, width), tok(0)),
            pl.BlockSpec((1, blk, width), tok(0)),
            pl.BlockSpec((1, blk, width), tok(1)),
            pl.BlockSpec((1, blk, width), tok(2)),
            pl.BlockSpec((1, LANES), const),
            pl.BlockSpec((HGRN_CHUNK, HGRN_CHUNK), const),
            pl.BlockSpec((LANES, LANES), const),
        ],
        out_specs=pl.BlockSpec((1, blk, width), tok(0)),
        out_shape=jax.ShapeDtypeStruct((bsz, seq, d), BF16),
        scratch_shapes=[pltpu.VMEM((heads, LANES, LANES), F32)]
        + [pltpu.VMEM((heads, blk, LANES), F32)] * 4
        + [pltpu.VMEM((heads, HGRN_CHUNK, LANES), F32),
           pltpu.VMEM((heads, HGRN_CHUNK, HGRN_CHUNK), F32),
           pltpu.VMEM((heads, diag_rows, LANES), F32)],
        compiler_params=pltpu.CompilerParams(
            dimension_semantics=("arbitrary", "arbitrary", "arbitrary"),
            vmem_limit_bytes=VMEM_LIMIT),
        name="hgrn_recurrence",
    )(proj3, lf3, lk3, proj3, proj3, gnorm, tri, ones).reshape(bsz * seq, d)


def kernel(x, fox_w_in, fox_b_f, fox_w_out, hgrn_w_in, hgrn_lb_table, hgrn_gnorm, hgrn_w_out,
           ffn_w_in, ffn_w_out, norm_gains):
    bsz, seq, d = x.shape
    depth = norm_gains.shape[0]
    assert d == FOX_HEADS * FOX_HEAD_DIM == HGRN_HEADS * HGRN_HEAD_DIM
    assert seq % ROW_TILE == 0 and seq % (2 * ATTN_BLOCK) == 0 and seq % HGRN_BLOCK == 0
    assert ffn_w_out.shape[1] % FFN_COLS == 0
    gains = norm_gains.astype(F32).reshape(depth, 4, 1, d)
    h = x.reshape(bsz * seq, d).astype(F32)

    for layer in range(depth):
        j = layer // 2
        if layer % 2 == 0:
            w_in = fox_w_in[j]
            w_f = jnp.pad(w_in[:, 3 * d:].astype(F32), ((0, 0), (0, LANES - FOX_HEADS)))
            wf_hi = w_f.astype(BF16)
            wf_lo = (w_f - wf_hi.astype(F32)).astype(BF16)
            b_f = jnp.pad(fox_b_f[j].astype(F32), (0, LANES - FOX_HEADS)).reshape(1, LANES)
            w_bf = w_in.astype(BF16)
            qk, vt, aug = _fox_in_proj(h, gains[layer, 0], w_bf, w_bf, wf_hi, wf_lo, b_f, seq)
            mixed = _fox_attention(qk, vt, aug, bsz, seq, d)
            w_out = fox_w_out[j]
        else:
            proj, lf, lk = _hgrn_in_proj(h, gains[layer, 0], hgrn_w_in[j].astype(BF16),
                                         hgrn_lb_table.astype(F32), layer)
            mixed = _hgrn_recurrence(proj, lf, lk, hgrn_gnorm[j].astype(F32).reshape(1, -1),
                                     bsz, seq, d)
            w_out = hgrn_w_out[j]
        h = _mix_ffn(mixed, w_out.astype(BF16), ffn_w_in[layer].astype(BF16),
                     ffn_w_out[layer].astype(BF16), h, norm_gains[layer, 1:].astype(F32))
    return h.reshape(bsz, seq, d).astype(x.dtype)
```

```python
import functools

import numpy as np
import jax
import jax.numpy as jnp
from jax import lax
from jax.experimental import pallas as pl
from jax.experimental.pallas import tpu as pltpu

F32 = jnp.float32
BF16 = jnp.bfloat16

EPS = 1e-6
LANES = 128
FOX_HEADS = 16
FOX_HEAD_DIM = 64
HGRN_HEADS = 8
HGRN_HEAD_DIM = 128
HGRN_CHUNK = 64
HGRN_SUB = 16
HGRN_TILE = 8
AUG = 8
ONES_ROWS = 16
LOG2E = 1.4426950408889634

ROW_TILE = 512
ATTN_BLOCK = 512
ATTN_COLS = 256
HGRN_BLOCK = 512
HGRN_HEADS_PER_STEP = 8
FFN_COLS = 256
FFN_ROW_GROUPS = 2
IN_ROW_GROUPS = 2
VMEM_LIMIT = 56 * 1024 * 1024


def _dot(a, b):
    return jnp.dot(a, b, preferred_element_type=F32)


def _dot_nt(a, b):
    return lax.dot_general(a, b, (((1,), (1,)), ((), ())), preferred_element_type=F32)


def _dot_tn(a, b):
    return lax.dot_general(a, b, (((0,), (0,)), ((), ())), preferred_element_type=F32)


def _split3(x):
    hi = x.astype(BF16)
    r = x - hi.astype(F32)
    mid = r.astype(BF16)
    lo = (r - mid.astype(F32)).astype(BF16)
    return hi, mid, lo


def _rms(x, g):
    ms = jnp.mean(x * x, axis=-1, keepdims=True)
    return x * lax.rsqrt(ms + EPS) * g


def _fox_in_kernel(x_ref, g_ref, w_ref, wvt_ref, wfh_ref, wfl_ref, bf_ref, tri_ref, psel_ref,
                   pone_ref, qk_ref, vt_ref, aug_ref, carry_s, *, tiles_per_seq, q_scale):
    d = x_ref.shape[1]

    @pl.when(pl.program_id(0) % tiles_per_seq == 0)
    def _():
        carry_s[...] = jnp.zeros_like(carry_s)

    hn = _rms(x_ref[...], g_ref[...])
    hn_hi = hn.astype(BF16)
    hn_lo = (hn - hn_hi.astype(F32)).astype(BF16)
    wfh = wfh_ref[...]
    z = _dot(hn_hi, wfh) + _dot(hn_lo, wfh) + _dot(hn_hi, wfl_ref[...]) + bf_ref[...]
    qk_ref[:, :d] = (_dot(hn_hi, w_ref[:, :d]) * q_scale).astype(qk_ref.dtype)

    logf = jnp.minimum(z, 0.0) - jnp.log1p(jnp.exp(-jnp.abs(z)))
    tri = tri_ref[...]
    l_hi, l_mid, l_lo = _split3(logf)
    c = _dot(tri, l_hi) + _dot(tri, l_mid) + _dot(tri, l_lo) + carry_s[...]
    carry_s[...] = c[c.shape[0] - 1:, :]
    qk_ref[:, d:] = _dot(hn_hi, w_ref[:, d:]).astype(qk_ref.dtype)

    c_hi, c_mid, c_lo = _split3(c * LOG2E)
    ccat = jnp.concatenate([c_hi, c_mid, c_lo], axis=1)
    aug_ref[...] = (_dot(ccat, psel_ref[...]) + pone_ref[...]).astype(aug_ref.dtype)
    vt = lax.dot_general(wvt_ref[...], hn_hi, (((0,), (1,)), ((), ())), preferred_element_type=F32)
    vt_ref[...] = vt.astype(vt_ref.dtype)


def _aug_tables():
    sel = np.zeros((3 * LANES, 2 * LANES), np.float32)
    one = np.zeros((1, 2 * LANES), np.float32)
    for h in range(FOX_HEADS):
        for part in range(3):
            sel[part * LANES + h, AUG * h + part] = 1.0
            sel[part * LANES + h, LANES + AUG * h + 3 + part] = -1.0
            one[0, AUG * h + 3 + part] = 1.0
            one[0, LANES + AUG * h + part] = 1.0
    return jnp.asarray(sel, BF16), jnp.asarray(one, F32)


def _fox_in_proj(h, gain, w_qk, w_vt, wf_hi, wf_lo, b_f, seq):
    m, d = h.shape
    tm = ROW_TILE
    tri = jnp.asarray(np.tril(np.ones((tm, tm), np.float32)), BF16)
    psel, pone = _aug_tables()
    const = lambda i: (0, 0)
    row = lambda i: (i, 0)
    resident = pl.Buffered(1)
    return pl.pallas_call(
        functools.partial(_fox_in_kernel, tiles_per_seq=seq // tm,
                          q_scale=FOX_HEAD_DIM ** -0.5 * LOG2E),
        grid=(m // tm,),
        in_specs=[
            pl.BlockSpec((tm, d), row),
            pl.BlockSpec((1, d), const),
            pl.BlockSpec((d, 2 * d), const, pipeline_mode=resident),
            pl.BlockSpec((d, d), lambda i: (0, 2), pipeline_mode=resident),
            pl.BlockSpec((d, LANES), const),
            pl.BlockSpec((d, LANES), const),
            pl.BlockSpec((1, LANES), const),
            pl.BlockSpec((tm, tm), const),
            pl.BlockSpec((3 * LANES, 2 * LANES), const),
            pl.BlockSpec((1, 2 * LANES), const),
        ],
        out_specs=[
            pl.BlockSpec((tm, 2 * d), row),
            pl.BlockSpec((d, tm), lambda i: (0, i)),
            pl.BlockSpec((tm, 2 * LANES), row),
        ],
        out_shape=[
            jax.ShapeDtypeStruct((m, 2 * d), BF16),
            jax.ShapeDtypeStruct((d, m), BF16),
            jax.ShapeDtypeStruct((m, 2 * LANES), BF16),
        ],
        scratch_shapes=[pltpu.VMEM((1, LANES), F32)],
        compiler_params=pltpu.CompilerParams(
            dimension_semantics=("arbitrary",), vmem_limit_bytes=VMEM_LIMIT),
        name="fox_in_proj",
    )(h, gain, w_qk, w_vt, wf_hi, wf_lo, b_f, tri, psel, pone)


def _fox_attn_kernel(q_ref, augq_ref, k_ref, augk_ref, vt_ref, o_ref,
                     qcat_s, s0_s, s1_s, mx0_s, mx1_s, m_s, acc_s, *, blk):
    hp = pl.program_id(1)
    g = pl.program_id(2)
    nq = 2 * blk
    hd = FOX_HEAD_DIM
    lane = lax.broadcasted_iota(jnp.int32, (nq, LANES), 1)
    q2 = q_ref[0]
    aq = augq_ref[0]
    zero = jnp.zeros_like(q2)
    for half in range(2):
        qm = jnp.where(lane // hd == half, q2, zero)
        am = jnp.where(lane // AUG == 2 * hp + half, aq, zero)
        qcat_s[half] = jnp.concatenate([qm, am], axis=1)
    m_s[...] = jnp.full(m_s.shape, -jnp.inf, F32)
    acc_s[...] = jnp.zeros(acc_s.shape, F32)

    col = ATTN_COLS
    n_col = nq // col
    def key_rows(kb, nk=blk):
        k0 = pl.multiple_of(kb * blk, blk)
        return jnp.concatenate([k_ref[0, pl.ds(k0, nk), :], augk_ref[0, pl.ds(k0, nk), :]], axis=1)

    def value_rows(kb, nk=blk):
        k0 = pl.multiple_of(kb * blk, blk)
        ones_rows = jnp.ones((ONES_ROWS, nk), BF16)
        return [jnp.concatenate([vt_ref[hd * half:hd * (half + 1), pl.ds(k0, nk)], ones_rows], axis=0)
                for half in range(2)]

    def visible_keys(diag_offset):
        return min(blk, diag_offset + col)

    def score_tile(kcat, s_ref, mx_ref, half, n):
        cols = slice(n * col, (n + 1) * col)
        nk = kcat.shape[0]
        st = _dot_nt(kcat, qcat_s[half, cols, :])
        s_ref[half, :nk, cols] = st
        if mx_ref is not None:
            mx_ref[half, :, cols] = jnp.max(st, axis=0, keepdims=True)

    def update_tile(vaug, s_ref, mx_ref, half, n, diag_offset=None):
        cols = slice(n * col, (n + 1) * col)
        nk = vaug[half].shape[1]
        st = s_ref[half, :nk, cols]
        if mx_ref is None:
            if diag_offset < blk - 1:
                krow = lax.broadcasted_iota(jnp.int32, (nk, col), 0)
                qcol = lax.broadcasted_iota(jnp.int32, (nk, col), 1)
                st = jnp.where(krow - qcol <= diag_offset, st, -jnp.inf)
            bmax = jnp.max(st, axis=0, keepdims=True)
        else:
            bmax = mx_ref[half, :, cols]
        m_prev = m_s[half, :, cols]
        m_new = jnp.maximum(m_prev, bmax)
        alpha = jnp.exp2(m_prev - m_new)
        pt = jnp.exp2(st - m_new).astype(BF16)
        acc_s[half, :, cols] = alpha * acc_s[half, :, cols] + _dot(vaug[half], pt)
        m_s[half, :, cols] = m_new

    def sweep(kb_scores, s_new, mx_new, kb_update, s_cur, mx_cur):
        kcat = key_rows(kb_scores)
        vaug = value_rows(kb_update)
        for n in range(n_col):
            for half in range(2):
                score_tile(kcat, s_new, mx_new, half, n)
                update_tile(vaug, s_cur, mx_cur, half, n)

    kcat = key_rows(0)
    for half in range(2):
        for n in range(n_col):
            score_tile(kcat, s0_s, mx0_s, half, n)

    def pair(t, carry):
        sweep(2 * t + 1, s1_s, mx1_s, 2 * t, s0_s, mx0_s)
        sweep(2 * t + 2, s0_s, mx0_s, 2 * t + 1, s1_s, mx1_s)
        return carry

    lax.fori_loop(0, g, pair, 0)
    sizes = sorted({visible_keys(n * col) for n in range(blk // col)})
    kcat = {nk: key_rows(2 * g + 1, nk) for nk in sizes}
    vaug = {nk: value_rows(2 * g, nk) for nk in sizes}
    for half in range(2):
        for n in range(n_col):
            if n * col >= blk:
                score_tile(kcat[visible_keys(n * col - blk)], s1_s, None, half, n)
            update_tile(vaug[visible_keys(n * col)], s0_s, None, half, n, diag_offset=n * col)
    vaug = {nk: value_rows(2 * g + 1, nk) for nk in sizes}
    for half in range(2):
        for n in range(blk // col, n_col):
            off = n * col - blk
            update_tile(vaug[visible_keys(off)], s1_s, None, half, n, diag_offset=off)

    outs = []
    for half in range(2):
        acc = acc_s[half]
        outs.append(acc[:hd] / acc[hd:hd + 1])
    o_ref[0] = jnp.concatenate(outs, axis=0).T.astype(o_ref.dtype)


def _fox_attention(qk, vt, aug, bsz, seq, d):
    blk = ATTN_BLOCK
    nq = 2 * blk
    n_pair = d // LANES
    n_step = seq // nq
    qk3 = qk.reshape(bsz, seq, 2 * d)
    aug3 = aug.reshape(bsz, seq, 2 * LANES)
    return pl.pallas_call(
        functools.partial(_fox_attn_kernel, blk=blk),
        grid=(bsz, n_pair, n_step),
        in_specs=[
            pl.BlockSpec((1, nq, LANES), lambda b, hp, g: (b, g, hp)),
            pl.BlockSpec((1, nq, LANES), lambda b, hp, g: (b, g, 0)),
            pl.BlockSpec((1, seq, LANES), lambda b, hp, g: (b, 0, n_pair + hp)),
            pl.BlockSpec((1, seq, LANES), lambda b, hp, g: (b, 0, 1)),
            pl.BlockSpec((LANES, seq), lambda b, hp, g: (hp, b)),
        ],
        out_specs=pl.BlockSpec((1, nq, LANES), lambda b, hp, g: (b, g, hp)),
        out_shape=jax.ShapeDtypeStruct((bsz, seq, d), BF16),
        scratch_shapes=[
            pltpu.VMEM((2, nq, 2 * LANES), BF16),
            pltpu.VMEM((2, blk, nq), F32),
            pltpu.VMEM((2, blk, nq), F32),
            pltpu.VMEM((2, 1, nq), F32),
            pltpu.VMEM((2, 1, nq), F32),
            pltpu.VMEM((2, 1, nq), F32),
            pltpu.VMEM((2, FOX_HEAD_DIM + ONES_ROWS, nq), F32),
        ],
        compiler_params=pltpu.CompilerParams(
            dimension_semantics=("arbitrary", "arbitrary", "arbitrary"),
            vmem_limit_bytes=VMEM_LIMIT),
        name="fox_attention",
    )(qk3, aug3, qk3, aug3, vt).reshape(bsz * seq, d)


def _mix_ffn_kernel(a_ref, wm_ref, wg_ref, wu_ref, wo_ref, h_ref, gains_ref, out_ref, *, d_ff):
    tm = a_ref.shape[0]
    rows = tm // FFN_ROW_GROUPS
    g_post, g_pre, g_ffn = gains_ref[0:1, :], gains_ref[1:2, :], gains_ref[2:3, :]
    groups = [slice(r * rows, (r + 1) * rows) for r in range(FFN_ROW_GROUPS)]
    y = [_dot(a_ref[rs, :], wm_ref[...]) for rs in groups]
    h1, acc = [], []
    for r, rs in enumerate(groups):
        h1.append(h_ref[rs, :] + _rms(y[r], g_post))
        x = _rms(h1[r], g_pre).astype(BF16)

        def gate_up(c0):
            return _dot(x, wg_ref[:, c0:c0 + FFN_COLS]), _dot(x, wu_ref[:, c0:c0 + FFN_COLS])

        part_sum = None
        nxt = gate_up(0)
        for c0 in range(0, d_ff, FFN_COLS):
            gate, up = nxt
            if c0 + FFN_COLS < d_ff:
                nxt = gate_up(c0 + FFN_COLS)
            act = (gate * jax.nn.sigmoid(gate) * up).astype(BF16)
            part = _dot(act, wo_ref[c0:c0 + FFN_COLS, :])
            part_sum = part if part_sum is None else part_sum + part
        acc.append(part_sum)
    for r, rs in enumerate(groups):
        out_ref[rs, :] = h1[r] + _rms(acc[r], g_ffn)


def _mix_ffn(a, w_mix, w_in, w_out, layer, h, gains):
    m, d = h.shape
    d_ff = w_out.shape[1]
    tm = ROW_TILE
    row = lambda i: (i, 0)
    const = lambda i: (0, 0)
    resident = pl.Buffered(1)
    return pl.pallas_call(
        functools.partial(_mix_ffn_kernel, d_ff=d_ff),
        grid=(m // tm,),
        in_specs=[
            pl.BlockSpec((tm, d), row),
            pl.BlockSpec((d, d), const, pipeline_mode=resident),
            pl.BlockSpec((None, d, d_ff), lambda i: (layer, 0, 0), pipeline_mode=resident),
            pl.BlockSpec((None, d, d_ff), lambda i: (layer, 0, 1), pipeline_mode=resident),
            pl.BlockSpec((None, d_ff, d), lambda i: (layer, 0, 0), pipeline_mode=resident),
            pl.BlockSpec((tm, d), row),
            pl.BlockSpec((3, d), const),
        ],
        out_specs=pl.BlockSpec((tm, d), row),
        out_shape=jax.ShapeDtypeStruct((m, d), F32),
        compiler_params=pltpu.CompilerParams(
            dimension_semantics=("arbitrary",), vmem_limit_bytes=VMEM_LIMIT),
        name="mix_ffn",
    )(a, w_mix, w_in, w_in, w_out, h, gains)


def _hgrn_in_kernel(x_ref, g_ref, w_ref, tab_ref, proj_ref, lf_ref, lk_ref, *, layer):
    tm, d = x_ref.shape
    rows = tm // IN_ROW_GROUPS
    tab = tab_ref[...]
    e = jnp.exp(tab - jnp.max(tab, axis=0, keepdims=True))
    lb = jnp.sum(e[:layer + 1], axis=0, keepdims=True) / jnp.sum(e, axis=0, keepdims=True)
    for r in range(IN_ROW_GROUPS):
        rs = slice(r * rows, (r + 1) * rows)
        hn = _rms(x_ref[rs, :], g_ref[...]).astype(BF16)
        z = _dot(hn, w_ref[:, d:2 * d])
        f = lb + (1.0 - lb) * jax.nn.sigmoid(z)
        lf_ref[rs, :] = jnp.log2(f)
        lk_ref[rs, :] = jnp.log2(1.0 - f)
        for dst, src in enumerate((0, 2, 3)):
            proj_ref[rs, dst * d:(dst + 1) * d] = _dot(
                hn, w_ref[:, src * d:(src + 1) * d]).astype(proj_ref.dtype)


def _hgrn_in_proj(h, gain, w, table, layer):
    m, d = h.shape
    tm = ROW_TILE
    const = lambda i: (0, 0)
    row = lambda i: (i, 0)
    return pl.pallas_call(
        functools.partial(_hgrn_in_kernel, layer=layer),
        grid=(m // tm,),
        in_specs=[
            pl.BlockSpec((tm, d), row),
            pl.BlockSpec((1, d), const),
            pl.BlockSpec(w.shape, const, pipeline_mode=pl.Buffered(1)),
            pl.BlockSpec(table.shape, const),
        ],
        out_specs=[pl.BlockSpec((tm, 3 * d), row), pl.BlockSpec((tm, d), row),
                   pl.BlockSpec((tm, d), row)],
        out_shape=[
            jax.ShapeDtypeStruct((m, 3 * d), BF16),
            jax.ShapeDtypeStruct((m, d), F32),
            jax.ShapeDtypeStruct((m, d), F32),
        ],
        compiler_params=pltpu.CompilerParams(
            dimension_semantics=("arbitrary",), vmem_limit_bytes=VMEM_LIMIT),
        name="hgrn_in_proj",
    )(h, gain, w, table)


def _hgrn_kernel(q_ref, lf_ref, lk_ref, i_ref, g_ref, gn_ref, tri_ref, ones_ref, o_ref,
                 st_s, b_s, bk_s, v_s, q_s, oi_s, ao_s, y_s, *, blk, heads):
    ch, sub = HGRN_CHUNK, HGRN_SUB
    tile = HGRN_TILE
    n_chunk = blk // ch
    n_sub = ch // sub
    width = heads * LANES

    @pl.when(pl.program_id(2) == 0)
    def _():
        st_s[...] = jnp.zeros_like(st_s)

    lcat = jnp.concatenate([lf_ref[0, c * ch:(c + 1) * ch, :] for c in range(n_chunk)], axis=1)
    l_hi = lcat.astype(BF16)
    l_lo = (lcat - l_hi.astype(F32)).astype(BF16)
    tri = tri_ref[...]
    bcat = _dot(tri, l_hi) + _dot(tri, l_lo)
    logk = lk_ref[0]
    for hh in range(heads):
        ln = slice(hh * LANES, (hh + 1) * LANES)
        v_s[hh] = i_ref[0, :, ln].astype(F32)
        q_s[hh] = q_ref[0, :, ln].astype(F32)
        for c in range(n_chunk):
            bc = bcat[:, c * width + hh * LANES:c * width + (hh + 1) * LANES]
            b_s[hh, c * ch:(c + 1) * ch, :] = bc
            bk_s[hh, c * ch:(c + 1) * ch, :] = bc - logk[c * ch:(c + 1) * ch, ln]

    ones = ones_ref[...]
    gn = gn_ref[...]
    row = lax.broadcasted_iota(jnp.int32, (tile, LANES), 0)
    zero_tile = jnp.zeros((tile, LANES), F32)

    def rows(ref, hh, r, n):
        return ref[hh, pl.ds(r, n), :]

    def first_products(r0, hh):
        st = st_s[hh]
        v_bf = rows(v_s, hh, r0, ch).astype(BF16)
        b_last = rows(b_s, hh, r0 + ch - 1, 1)
        o_inter = _dot_nt((rows(q_s, hh, r0, ch) * jnp.exp2(rows(b_s, hh, r0, ch))).astype(BF16),
                          st.astype(BF16))
        k_dec = jnp.exp2(b_last - rows(bk_s, hh, r0, ch)).astype(BF16)
        st_new = jnp.exp2(b_last) * st + _dot_tn(v_bf, k_dec)

        a_rows = [jnp.zeros((sub, ch), F32)]
        for i in range(1, n_sub):
            ref = rows(b_s, hh, r0 + sub * i - 1, 1)
            qt = rows(q_s, hh, r0 + sub * i, sub) * jnp.exp2(rows(b_s, hh, r0 + sub * i, sub) - ref)
            kt = jnp.concatenate([jnp.exp2(ref - rows(bk_s, hh, r0, sub * i)),
                                  jnp.zeros((ch - sub * i, LANES), F32)], axis=0)
            a_rows.append(_dot_nt(qt.astype(BF16), kt.astype(BF16)))

        xs = []
        for i in range(n_sub):
            t0 = r0 + sub * i
            b_lo, b_hi = rows(b_s, hh, t0, tile), rows(b_s, hh, t0 + tile, tile)
            q_lo, q_hi = rows(q_s, hh, t0, tile), rows(q_s, hh, t0 + tile, tile)
            for s in range(sub):
                bks = rows(bk_s, hh, t0 + s, 1)
                if s < tile:
                    xs.append(q_lo * jnp.exp2(jnp.where(row >= s, b_lo - bks, -jnp.inf)))
                    xs.append(q_hi * jnp.exp2(b_hi - bks))
                else:
                    xs.append(q_hi * jnp.exp2(jnp.where(row >= s - tile, b_hi - bks, -jnp.inf)))
        y = _dot(jnp.concatenate(xs, axis=0).astype(BF16), ones)
        st_s[hh] = st_new
        oi_s[hh] = o_inter
        ao_s[hh] = jnp.concatenate(a_rows, axis=0)
        y_s[hh] = y

    def second_products(r0, hh):
        v_bf = rows(v_s, hh, r0, ch).astype(BF16)
        o = oi_s[hh] + _dot(ao_s[hh].astype(BF16), v_bf)
        o_diag = []
        n = 0
        for i in range(n_sub):
            acc_lo, acc_hi = zero_tile, zero_tile
            for s in range(sub):
                vs = rows(v_s, hh, r0 + sub * i + s, 1)
                if s < tile:
                    acc_lo = acc_lo + y_s[hh, n * tile:(n + 1) * tile, :] * vs
                    n += 1
                acc_hi = acc_hi + y_s[hh, n * tile:(n + 1) * tile, :] * vs
                n += 1
            o_diag += [acc_lo, acc_hi]
        o = o + jnp.concatenate(o_diag, axis=0)
        ln = slice(hh * LANES, (hh + 1) * LANES)
        g = g_ref[0, pl.ds(r0, ch), ln].astype(F32)
        o_ref[0, pl.ds(r0, ch), ln] = (_rms(o, gn) * (g * jax.nn.sigmoid(g))).astype(o_ref.dtype)

    for hh in range(heads):
        first_products(0, hh)

    def chunk(c, carry):
        r0 = pl.multiple_of(c * ch, ch)
        r_prev = pl.multiple_of((c - 1) * ch, ch)
        for hh in range(heads):
            second_products(r_prev, hh)
        for hh in range(heads):
            first_products(r0, hh)
        return carry

    lax.fori_loop(1, n_chunk, chunk, 0)
    for hh in range(heads):
        second_products(blk - ch, hh)


def _hgrn_recurrence(proj, lf, lk, gnorm, bsz, seq, d):
    blk = HGRN_BLOCK
    heads = HGRN_HEADS_PER_STEP
    groups = d // (heads * LANES)
    width = heads * LANES
    tiles_per_sub = HGRN_SUB + HGRN_SUB // 2
    diag_rows = (HGRN_CHUNK // HGRN_SUB) * tiles_per_sub * HGRN_TILE
    proj3 = proj.reshape(bsz, seq, 3 * d)
    lf3 = lf.reshape(bsz, seq, d)
    lk3 = lk.reshape(bsz, seq, d)
    tri = jnp.asarray(np.tril(np.ones((HGRN_CHUNK, HGRN_CHUNK), np.float32)), BF16)
    ones = jnp.ones((LANES, LANES), BF16)
    const = lambda b, h, t: (0, 0)
    tok = lambda col: (lambda b, h, t: (b, t, col * groups + h))
    return pl.pallas_call(
        functools.partial(_hgrn_kernel, blk=blk, heads=heads),
        grid=(bsz, groups, seq // blk),
        in_specs=[
            pl.BlockSpec((1, blk, width), tok(0)),
            pl.BlockSpec((1, blk, width), tok(0)),
            pl.BlockSpec((1, blk, width), tok(0)),
            pl.BlockSpec((1, blk, width), tok(1)),
            pl.BlockSpec((1, blk, width), tok(2)),
            pl.BlockSpec((1, LANES), const),
            pl.BlockSpec((HGRN_CHUNK, HGRN_CHUNK), const),
            pl.BlockSpec((LANES, LANES), const),
        ],
        out_specs=pl.BlockSpec((1, blk, width), tok(0)),
        out_shape=jax.ShapeDtypeStruct((bsz, seq, d), BF16),
        scratch_shapes=[pltpu.VMEM((heads, LANES, LANES), F32)]
        + [pltpu.VMEM((heads, blk, LANES), F32)] * 4
        + [pltpu.VMEM((heads, HGRN_CHUNK, LANES), F32),
           pltpu.VMEM((heads, HGRN_CHUNK, HGRN_CHUNK), F32),
           pltpu.VMEM((heads, diag_rows, LANES), F32)],
        compiler_params=pltpu.CompilerParams(
            dimension_semantics=("arbitrary", "arbitrary", "arbitrary"),
            vmem_limit_bytes=VMEM_LIMIT),
        name="hgrn_recurrence",
    )(proj3, lf3, lk3, proj3, proj3, gnorm, tri, ones).reshape(bsz * seq, d)


def kernel(x, fox_w_in, fox_b_f, fox_w_out, hgrn_w_in, hgrn_lb_table, hgrn_gnorm, hgrn_w_out,
           ffn_w_in, ffn_w_out, norm_gains):
    bsz, seq, d = x.shape
    depth = norm_gains.shape[0]
    assert d == FOX_HEADS * FOX_HEAD_DIM == HGRN_HEADS * HGRN_HEAD_DIM
    assert seq % ROW_TILE == 0 and seq % (2 * ATTN_BLOCK) == 0 and seq % HGRN_BLOCK == 0
    assert ffn_w_out.shape[1] % FFN_COLS == 0
    gains = norm_gains.astype(F32).reshape(depth, 4, 1, d)
    h = x.reshape(bsz * seq, d).astype(F32)
    ffn_in_bf = ffn_w_in.astype(BF16)
    ffn_out_bf = ffn_w_out.astype(BF16)

    for layer in range(depth):
        j = layer // 2
        if layer % 2 == 0:
            w_in = fox_w_in[j]
            w_f = jnp.pad(w_in[:, 3 * d:].astype(F32), ((0, 0), (0, LANES - FOX_HEADS)))
            wf_hi = w_f.astype(BF16)
            wf_lo = (w_f - wf_hi.astype(F32)).astype(BF16)
            b_f = jnp.pad(fox_b_f[j].astype(F32), (0, LANES - FOX_HEADS)).reshape(1, LANES)
            w_bf = w_in.astype(BF16)
            qk, vt, aug = _fox_in_proj(h, gains[layer, 0], w_bf, w_bf, wf_hi, wf_lo, b_f, seq)
            mixed = _fox_attention(qk, vt, aug, bsz, seq, d)
            w_out = fox_w_out[j]
        else:
            proj, lf, lk = _hgrn_in_proj(h, gains[layer, 0], hgrn_w_in[j].astype(BF16),
                                         hgrn_lb_table.astype(F32), layer)
            mixed = _hgrn_recurrence(proj, lf, lk, hgrn_gnorm[j].astype(F32).reshape(1, -1),
                                     bsz, seq, d)
            w_out = hgrn_w_out[j]
        h = _mix_ffn(mixed, w_out.astype(BF16), ffn_in_bf, ffn_out_bf, layer, h,
                     norm_gains[layer, 1:].astype(F32))
    return h.reshape(bsz, seq, d).astype(x.dtype)
```
